```python
import math
import jax, jax.numpy as jnp
from jax import lax
import numpy as np

D_MODEL = 1024
BATCH = 1
SEQ = 16384
DEPTH = 4
DEC_BATCH = 32
DEC_SEQ = 64
PAST_LEN = 1024

CHUNK = 64
CONV_K = 4
N_BRANCH = 3
RMS_EPS = 1e-6
D_POOL = D_MODEL
POOL_GROUPS = 4
POOL_GC = D_POOL // POOL_GROUPS
POOL_WINDOWS = (2, 4, 8, 16)
POOL_HIST = 15
D_SSD = D_MODEL
SSD_P = 64
SSD_H = D_SSD // SSD_P
SSD_G = 2
SSD_N = 128
SSD_CONV_DIM = D_SSD + 2 * SSD_G * SSD_N
GDN_DK = 128
GDN_DV = 128
GDN_HV = D_MODEL // GDN_DV
GDN_HK = GDN_HV // 2
GDN_CONV_DIM = 2 * GDN_HK * GDN_DK + GDN_HV * GDN_DV
IN_SPLIT = (D_POOL, D_POOL,
            D_SSD, D_SSD, SSD_G * SSD_N, SSD_G * SSD_N, SSD_H,
            GDN_HK * GDN_DK, GDN_HK * GDN_DK, GDN_HV * GDN_DV, GDN_HV * GDN_DV, GDN_HV, GDN_HV,
            N_BRANCH * D_MODEL)
IN_WIDTH = sum(IN_SPLIT)

kernel_name = 'hybrid_pool_ssd_gdn_stream_step'


def rmsnorm(x, w):
    xf = x.astype(jnp.float32)
    y = xf * lax.rsqrt(jnp.mean(xf * xf, axis=-1, keepdims=True) + RMS_EPS)
    return (y * w.astype(jnp.float32)).astype(x.dtype)


def l2norm(x):
    return x * lax.rsqrt(jnp.sum(x * x, axis=-1, keepdims=True) + 1e-6)


def split_cols(p):
    outs = []
    off = 0
    for n in IN_SPLIT:
        outs.append(p[..., off:off + n])
        off += n
    return outs


def causal_dwconv(u, hist, w):
    full = jnp.concatenate([hist.astype(u.dtype), u], axis=1)
    out = lax.conv_general_dilated(full, w[:, None, :].astype(u.dtype), window_strides=(1,),
                                   padding='VALID', dimension_numbers=('NWC', 'WIO', 'NWC'),
                                   feature_group_count=u.shape[-1])
    return out, full[:, -(CONV_K - 1):]


def pool_mixer(a, hist, pos0, pool_w, pool_scale):
    bsz, L, C = a.shape
    full = jnp.concatenate([hist.astype(a.dtype), a], axis=1)
    ff = full.astype(jnp.float32)
    cs = jnp.concatenate([jnp.zeros((bsz, 1, C), jnp.float32), jnp.cumsum(ff, axis=1)], axis=1)
    pos = (pos0 + jnp.arange(L)).astype(jnp.float32)
    off = POOL_HIST + 1
    means = []
    for gi, win in enumerate(POOL_WINDOWS):
        sl = slice(gi * POOL_GC, (gi + 1) * POOL_GC)
        wsum = cs[:, off:, sl] - cs[:, off - win:off - win + L, sl]
        cnt = jnp.minimum(pos + 1.0, float(win))
        means.append(wsum / cnt[None, :, None])
    d = jnp.concatenate(means, axis=-1) - ff[:, POOL_HIST:]
    y = jnp.einsum('blgc,gcd->blgd', d.reshape(bsz, L, POOL_GROUPS, POOL_GC), pool_w.astype(jnp.float32))
    y = y.reshape(bsz, L, C) * pool_scale.astype(jnp.float32)
    return y, full[:, -POOL_HIST:]


def ssd_scan(x, dt, A, Bm, Cm, S0):
    bsz, L, H, P = x.shape
    G, N = Bm.shape[-2], Bm.shape[-1]
    E = H // G
    Q = min(CHUNK, L)
    nc = L // Q
    xc = x.reshape(bsz, nc, Q, G, E, P)
    dtc = dt.reshape(bsz, nc, Q, G, E)
    Bc = Bm.reshape(bsz, nc, Q, G, N)
    Cc = Cm.reshape(bsz, nc, Q, G, N)
    acum = jnp.cumsum(dtc * A.reshape(G, E), axis=2)
    causal = jnp.tril(jnp.ones((Q, Q), bool))
    seg = acum[:, :, :, None] - acum[:, :, None, :]
    Lm = jnp.exp(jnp.where(causal[:, :, None, None], seg, -jnp.inf))
    CB = jnp.einsum('bcqgn,bcsgn->bcqsg', Cc, Bc)
    xdt = xc * dtc[..., None]
    y_intra = jnp.einsum('bcqsg,bcqsge,bcsgep->bcqgep', CB, Lm, xdt)
    decay_out = jnp.exp(acum[:, :, -1:] - acum)
    chunk_states = jnp.einsum('bcsgn,bcsge,bcsgep->bcgepn', Bc, decay_out, xdt)
    chunk_decay = jnp.exp(acum[:, :, -1])

    def step(S, inp):
        dec, st = inp
        return S * dec[..., None, None] + st, S

    S_fin, S_enter = lax.scan(step, S0.reshape(bsz, G, E, P, N),
                              (jnp.moveaxis(chunk_decay, 1, 0), jnp.moveaxis(chunk_states, 1, 0)))
    S_enter = jnp.moveaxis(S_enter, 0, 1)
    y_inter = jnp.einsum('bcqgn,bcqge,bcgepn->bcqgep', Cc, jnp.exp(acum), S_enter)
    y = (y_intra + y_inter).reshape(bsz, L, H, P)
    return y, S_fin.reshape(bsz, H, P, N)


def gdn_chunk(q, k, v, beta, g, S0):
    bsz, L, H, dk = q.shape
    Q = min(CHUNK, L)
    nc = L // Q

    def blk(t):
        return jnp.moveaxis(t.reshape((bsz, nc, Q, H) + t.shape[3:]), 3, 2)

    qc, kc, vc, bc = blk(q), blk(k), blk(v), blk(beta)
    gc = jnp.cumsum(blk(g), axis=3)
    incl = jnp.tril(jnp.ones((Q, Q), bool))
    strict = jnp.tril(jnp.ones((Q, Q), bool), -1)
    decay = jnp.exp(jnp.where(incl, gc[..., :, None] - gc[..., None, :], -jnp.inf))
    kb = kc * bc[..., None]
    Lm = jnp.where(strict, jnp.einsum('bchqd,bchsd->bchqs', kb, kc) * decay, 0.0)
    Imat = Lm + jnp.eye(Q, dtype=jnp.float32)
    u = lax.linalg.triangular_solve(Imat, vc * bc[..., None], left_side=True, lower=True, unit_diagonal=True)
    w = lax.linalg.triangular_solve(Imat, kb * jnp.exp(gc)[..., None], left_side=True, lower=True, unit_diagonal=True)
    Aqk = jnp.einsum('bchqd,bchsd->bchqs', qc, kc) * decay
    qg = qc * jnp.exp(gc)[..., None]
    kdec = kc * jnp.exp(gc[..., -1:] - gc)[..., None]
    glast = jnp.exp(gc[..., -1])

    def step(S, inp):
        u_, w_, qg_, kdec_, A_, gl_ = inp
        vnew = u_ - jnp.einsum('bhqd,bhde->bhqe', w_, S)
        o = jnp.einsum('bhqd,bhde->bhqe', qg_, S) + jnp.einsum('bhqs,bhse->bhqe', A_, vnew)
        S = S * gl_[..., None, None] + jnp.einsum('bhqd,bhqe->bhde', kdec_, vnew)
        return S, o

    S_fin, o = lax.scan(step, S0, tuple(jnp.moveaxis(t, 1, 0) for t in (u, w, qg, kdec, Aqk, glast)))
    o = jnp.moveaxis(jnp.moveaxis(o, 0, 1), 2, 3).reshape(bsz, L, H, v.shape[-1])
    return o, S_fin


def trunk_layer(x, pool_hist, ssd_conv_hist, ssd_state, gdn_conv_hist, gdn_state, pos0,
                norm_w, w_in, pool_w, pool_scale, ssd_conv_w, ssd_conv_b, ssd_dt_bias, ssd_A_log, ssd_D,
                ssd_norm_w, gdn_conv_w, gdn_dt_bias, gdn_A_log, gdn_norm_w, w_br_pool, w_br_ssd, w_br_gdn, w_out):
    f32 = jnp.float32
    dty = x.dtype
    bsz, L, _ = x.shape
    h = rmsnorm(x, norm_w)
    proj = h @ w_in
    (a_pool, z_pool, x_ssd, z_ssd, b_ssd, c_ssd, dt_ssd,
     q_in, k_in, v_in, z_gdn, b_gdn, a_gdn, gate_l) = split_cols(proj)

    y_pool, new_pool = pool_mixer(a_pool, pool_hist, pos0, pool_w, pool_scale)
    y_pool = y_pool * jax.nn.silu(z_pool.astype(f32))

    xbc, new_ssd_conv = causal_dwconv(jnp.concatenate([x_ssd, b_ssd, c_ssd], axis=-1), ssd_conv_hist, ssd_conv_w)
    xbc = jax.nn.silu((xbc + ssd_conv_b).astype(f32))
    nbc = SSD_G * SSD_N
    xh = xbc[..., :D_SSD].reshape(bsz, L, SSD_H, SSD_P)
    Bm = xbc[..., D_SSD:D_SSD + nbc].reshape(bsz, L, SSD_G, SSD_N)
    Cm = xbc[..., D_SSD + nbc:].reshape(bsz, L, SSD_G, SSD_N)
    dt = jax.nn.softplus(dt_ssd.astype(f32) + ssd_dt_bias.astype(f32))
    A = -jnp.exp(ssd_A_log.astype(f32))
    y, S_ssd = ssd_scan(xh, dt, A, Bm, Cm, ssd_state.astype(f32))
    y = y + ssd_D.astype(f32)[:, None] * xh
    yz = (y.reshape(bsz, L, D_SSD) * jax.nn.silu(z_ssd.astype(f32))).reshape(bsz, L, SSD_G, D_SSD // SSD_G)
    yz = yz * lax.rsqrt(jnp.mean(yz * yz, axis=-1, keepdims=True) + RMS_EPS)
    y_ssd = yz.reshape(bsz, L, D_SSD) * ssd_norm_w.astype(f32)

    qkv, new_gdn_conv = causal_dwconv(jnp.concatenate([q_in, k_in, v_in], axis=-1), gdn_conv_hist, gdn_conv_w)
    qkv = jax.nn.silu(qkv.astype(f32))
    nk = GDN_HK * GDN_DK
    rep = GDN_HV // GDN_HK
    q = jnp.repeat(l2norm(qkv[..., :nk].reshape(bsz, L, GDN_HK, GDN_DK)), rep, axis=2) * (GDN_DK ** -0.5)
    k = jnp.repeat(l2norm(qkv[..., nk:2 * nk].reshape(bsz, L, GDN_HK, GDN_DK)), rep, axis=2)
    v = qkv[..., 2 * nk:].reshape(bsz, L, GDN_HV, GDN_DV)
    beta = jax.nn.sigmoid(b_gdn.astype(f32))
    g = -jnp.exp(gdn_A_log.astype(f32)) * jax.nn.softplus(a_gdn.astype(f32) + gdn_dt_bias.astype(f32))
    o, S_gdn = gdn_chunk(q, k, v, beta, g, gdn_state.astype(f32))
    o = rmsnorm(o, gdn_norm_w) * jax.nn.silu(z_gdn.astype(f32).reshape(bsz, L, GDN_HV, GDN_DV))
    y_gdn = o.reshape(bsz, L, GDN_HV * GDN_DV)

    gates = jax.nn.sigmoid(gate_l.astype(f32)).reshape(bsz, L, N_BRANCH, D_MODEL)
    merged = (gates[:, :, 0] * (y_pool.astype(dty) @ w_br_pool).astype(f32)
              + gates[:, :, 1] * (y_ssd.astype(dty) @ w_br_ssd).astype(f32)
              + gates[:, :, 2] * (y_gdn.astype(dty) @ w_br_gdn).astype(f32))
    x_new = x + (merged.astype(dty) @ w_out).astype(dty)
    return x_new, new_pool, new_ssd_conv, S_ssd.astype(dty), new_gdn_conv, S_gdn.astype(dty)


def setup_inputs(seed: int = 0) -> dict:
    key = jax.random.key(seed)
    ks = iter(jax.random.split(key, 40))

    def nrm(shape, scale):
        return scale * jax.random.normal(next(ks), shape, jnp.float32)

    def unif(shape, lo, hi):
        return jax.random.uniform(next(ks), shape, jnp.float32, lo, hi)

    def dt_bias(shape):
        dtv = jnp.exp(unif(shape, math.log(1e-3), math.log(1e-1)))
        return dtv + jnp.log(-jnp.expm1(-dtv))

    def gain(shape):
        return 1.0 + nrm(shape, 0.02)

    return {
        'x_prompt': nrm((BATCH, SEQ, D_MODEL), 1.0),
        'x_sample': nrm((DEC_BATCH, DEC_SEQ, D_MODEL), 1.0),
        'state_pool': nrm((DEPTH, DEC_BATCH, POOL_HIST, D_POOL), 1.0),
        'state_ssd_conv': nrm((DEPTH, DEC_BATCH, CONV_K - 1, SSD_CONV_DIM), 1.0),
        'state_ssd': nrm((DEPTH, DEC_BATCH, SSD_H, SSD_P, SSD_N), 0.1),
        'state_gdn_conv': nrm((DEPTH, DEC_BATCH, CONV_K - 1, GDN_CONV_DIM), 1.0),
        'state_gdn': nrm((DEPTH, DEC_BATCH, GDN_HV, GDN_DK, GDN_DV), 0.1),
        'norm_w': gain((DEPTH, D_MODEL)),
        'w_in': nrm((DEPTH, D_MODEL, IN_WIDTH), D_MODEL ** -0.5),
        'pool_w': nrm((DEPTH, POOL_GROUPS, POOL_GC, POOL_GC), POOL_GC ** -0.5),
        'pool_scale': gain((DEPTH, D_POOL)),
        'ssd_conv_w': nrm((DEPTH, CONV_K, SSD_CONV_DIM), CONV_K ** -0.5),
        'ssd_conv_b': nrm((DEPTH, SSD_CONV_DIM), 0.02),
        'ssd_dt_bias': dt_bias((DEPTH, SSD_H)),
        'ssd_A_log': jnp.log(unif((DEPTH, SSD_H), 1.0, 16.0)),
        'ssd_D': gain((DEPTH, SSD_H)),
        'ssd_norm_w': gain((DEPTH, D_SSD)),
        'gdn_conv_w': nrm((DEPTH, CONV_K, GDN_CONV_DIM), CONV_K ** -0.5),
        'gdn_dt_bias': dt_bias((DEPTH, GDN_HV)),
        'gdn_A_log': jnp.log(unif((DEPTH, GDN_HV), 1.0, 16.0)),
        'gdn_norm_w': gain((DEPTH, GDN_DV)),
        'w_br_pool': nrm((DEPTH, D_POOL, D_MODEL), D_POOL ** -0.5),
        'w_br_ssd': nrm((DEPTH, D_SSD, D_MODEL), D_SSD ** -0.5),
        'w_br_gdn': nrm((DEPTH, GDN_HV * GDN_DV, D_MODEL), (GDN_HV * GDN_DV) ** -0.5),
        'w_out': nrm((DEPTH, D_MODEL, D_MODEL), D_MODEL ** -0.5),
        'final_norm_w': gain((D_MODEL,)),
    }


def reference(x_prompt, x_sample, state_pool, state_ssd_conv, state_ssd, state_gdn_conv, state_gdn,
              norm_w, w_in, pool_w, pool_scale, ssd_conv_w, ssd_conv_b, ssd_dt_bias, ssd_A_log, ssd_D,
              ssd_norm_w, gdn_conv_w, gdn_dt_bias, gdn_A_log, gdn_norm_w, w_br_pool, w_br_ssd, w_br_gdn,
              w_out, final_norm_w):
    bp = x_prompt.shape[0]
    pdt = x_prompt.dtype
    hp, hs = x_prompt, x_sample
    pool_p, pool_s, sconv_p, sconv_s, ssd_p, ssd_s = [], [], [], [], [], []
    gconv_p, gconv_s, gdn_p, gdn_s = [], [], [], []
    for l in range(DEPTH):
        lw = (norm_w[l], w_in[l], pool_w[l], pool_scale[l], ssd_conv_w[l], ssd_conv_b[l], ssd_dt_bias[l],
              ssd_A_log[l], ssd_D[l], ssd_norm_w[l], gdn_conv_w[l], gdn_dt_bias[l], gdn_A_log[l], gdn_norm_w[l],
              w_br_pool[l], w_br_ssd[l], w_br_gdn[l], w_out[l])
        hp, a1, a2, a3, a4, a5 = trunk_layer(
            hp, jnp.zeros((bp, POOL_HIST, D_POOL), pdt), jnp.zeros((bp, CONV_K - 1, SSD_CONV_DIM), pdt),
            jnp.zeros((bp, SSD_H, SSD_P, SSD_N), pdt), jnp.zeros((bp, CONV_K - 1, GDN_CONV_DIM), pdt),
            jnp.zeros((bp, GDN_HV, GDN_DK, GDN_DV), pdt), 0, *lw)
        hs, b1, b2, b3, b4, b5 = trunk_layer(
            hs, state_pool[l], state_ssd_conv[l], state_ssd[l], state_gdn_conv[l], state_gdn[l], PAST_LEN, *lw)
        pool_p.append(a1); sconv_p.append(a2); ssd_p.append(a3); gconv_p.append(a4); gdn_p.append(a5)
        pool_s.append(b1); sconv_s.append(b2); ssd_s.append(b3); gconv_s.append(b4); gdn_s.append(b5)
    y_prompt = rmsnorm(hp, final_norm_w)
    y_sample = rmsnorm(hs, final_norm_w)
    new_pool_prompt = jnp.stack(pool_p)
    new_pool_sample = jnp.stack(pool_s)
    new_ssd_conv_prompt = jnp.stack(sconv_p)
    new_ssd_conv_sample = jnp.stack(sconv_s)
    new_ssd_prompt = jnp.stack(ssd_p)
    new_ssd_sample = jnp.stack(ssd_s)
    new_gdn_conv_prompt = jnp.stack(gconv_p)
    new_gdn_conv_sample = jnp.stack(gconv_s)
    new_gdn_prompt = jnp.stack(gdn_p)
    new_gdn_sample = jnp.stack(gdn_s)
    return (y_prompt, y_sample, new_pool_prompt, new_pool_sample, new_ssd_conv_prompt, new_ssd_conv_sample,
            new_ssd_prompt, new_ssd_sample, new_gdn_conv_prompt, new_gdn_conv_sample, new_gdn_prompt, new_gdn_sample)
```

```python
import functools

import jax
import jax.numpy as jnp
from jax import lax
from jax.experimental import pallas as pl
from jax.experimental.pallas import tpu as pltpu

F32 = jnp.float32
BF16 = jnp.bfloat16

D_MODEL = 1024
DEPTH = 4
CHUNK = 64
RMS_EPS = 1e-6
POOL_HIST = 15
CONV_HIST = 3
HEAD_ROWS = 16
SSD_H, SSD_P, SSD_N, SSD_G = 16, 64, 128, 2
SSD_CONV = 1536
GDN_HV, GDN_DK, GDN_DV = 8, 128, 128
GDN_CONV = 2048
SMALL_W = 128

OFF_POOL, OFF_SSD, OFF_SMALL, OFF_GDN, OFF_GATE, W_ALL = 0, 2048, 4608, 4736, 7808, 10880

PROMPT_BLOCK = 256
SAMPLE_SEQS = 2
VMEM_LIMIT_BYTES = 60 * 1024 * 1024


def _dot(a, b):
    return jnp.dot(a, b, preferred_element_type=F32)


def _dot_nt(a, b):
    return lax.dot_general(a, b, (((1,), (1,)), ((), ())), preferred_element_type=F32)


def _dot_tn(a, b):
    return lax.dot_general(a, b, (((0,), (0,)), ((), ())), preferred_element_type=F32)


def _silu(x):
    return x * jax.nn.sigmoid(x)


def _softplus(x):
    return jnp.maximum(x, 0.0) + jnp.log1p(jnp.exp(-jnp.abs(x)))


def _split3(a):
    hi = a.astype(BF16)
    r = a - hi.astype(F32)
    mid = r.astype(BF16)
    lo = (r - mid.astype(F32)).astype(BF16)
    return hi, mid, lo


def _pack3(a, lane_lt32):
    hi, mid, lo = _split3(a)
    z = jnp.zeros_like(a)
    p = (jnp.where(lane_lt32, hi.astype(F32), z)
         + pltpu.roll(jnp.where(lane_lt32, mid.astype(F32), z), 32, 1)
         + pltpu.roll(jnp.where(lane_lt32, lo.astype(F32), z), 64, 1))
    return p.astype(BF16)


def _causal_conv(full, w, k_taps):
    acc = full * w[k_taps - 1:k_taps, :]
    for j in range(1, k_taps):
        acc = acc + pltpu.roll(full, j, 0) * w[k_taps - 1 - j:k_taps - j, :]
    return acc


def _neumann_inverse(x, eye):
    p = eye + x
    xb = x.astype(BF16)
    y = _dot(xb, xb)
    for _ in range(4):
        yb = y.astype(BF16)
        r = _dot(jnp.concatenate([yb, p.astype(BF16)], axis=0), yb)
        y, p = r[:128], p + r[128:]
    return p + _dot(p.astype(BF16), y.astype(BF16))


def _layer_kernel(*refs, prompt, n_seq, seq_len, final_norm):
    tb = n_seq * seq_len
    n_chunks = tb // CHUNK
    chunks_per_seq = seq_len // CHUNK
    it = iter(refs)
    x_ref = next(it)
    if not prompt:
        pool_h, sconv_h, ssd_s, gconv_h, gdn_s = (next(it) for _ in range(5))
    (normw, wall, poolw, poolsc, sconvw, sconvb, sbias, salog, ssd_d, ssdnw, gconvw, gdnnw,
     wbr, wout, fnw, btri, e_dt, e_beta, e_g) = (next(it) for _ in range(19))
    xo, o_pool, o_sconv, o_ssd, o_gconv, o_gdn = (next(it) for _ in range(6))
    abuf, sbuf, gbuf, yssd, ygdn = (next(it) for _ in range(5))

    step = pl.program_id(0)
    if prompt:
        @pl.when(step == 0)
        def _init():
            abuf[:, 0:HEAD_ROWS, :] = jnp.zeros((n_seq, HEAD_ROWS, D_MODEL), F32)
            sbuf[:, 0:HEAD_ROWS, :] = jnp.zeros((n_seq, HEAD_ROWS, SSD_CONV), F32)
            gbuf[:, 0:HEAD_ROWS, :] = jnp.zeros((n_seq, HEAD_ROWS, GDN_CONV), F32)
            o_ssd[...] = jnp.zeros(o_ssd.shape, F32)
            o_gdn[...] = jnp.zeros(o_gdn.shape, F32)
    else:
        abuf[:, 0:HEAD_ROWS, :] = jnp.zeros((n_seq, HEAD_ROWS, D_MODEL), F32)
        sbuf[:, 0:HEAD_ROWS, :] = jnp.zeros((n_seq, HEAD_ROWS, SSD_CONV), F32)
        gbuf[:, 0:HEAD_ROWS, :] = jnp.zeros((n_seq, HEAD_ROWS, GDN_CONV), F32)
        abuf[:, HEAD_ROWS - POOL_HIST:HEAD_ROWS, :] = pool_h[...]
        sbuf[:, HEAD_ROWS - CONV_HIST:HEAD_ROWS, :] = sconv_h[...]
        gbuf[:, HEAD_ROWS - CONV_HIST:HEAD_ROWS, :] = gconv_h[...]

    x = x_ref[...]
    hn = (x * lax.rsqrt(jnp.mean(x * x, axis=-1, keepdims=True) + RMS_EPS) * normw[...]).astype(BF16)

    pp = _dot(hn, wall[:, OFF_POOL:OFF_SSD])
    a_in, z_pool = pp[:, :D_MODEL], pp[:, D_MODEL:]
    row16 = lax.broadcasted_iota(jnp.int32, (HEAD_ROWS, 256), 0).astype(F32)
    y_pool_parts = []
    for s in range(n_seq):
        rs = slice(s * seq_len, (s + 1) * seq_len)
        abuf[s, HEAD_ROWS:, :] = a_in[rs]
        full = abuf[s]
        o_pool[s] = full[seq_len + HEAD_ROWS - POOL_HIST:, :]
        if prompt:
            abuf[s, 0:HEAD_ROWS, :] = full[seq_len:, :]
        ys = []
        for g in range(4):
            win = 2 << g
            fg = full[:, 256 * g:256 * (g + 1)]
            acc = fg
            for k in range(g + 1):
                acc = acc + pltpu.roll(acc, 1 << k, 0)
            wsum = acc[HEAD_ROWS:]
            cur = fg[HEAD_ROWS:]
            inv_w = 1.0 / win
            if prompt:
                fac = jnp.where(step == 0, 1.0 / jnp.minimum(row16 + 1.0, float(win)), inv_w)
                d = jnp.concatenate([wsum[:HEAD_ROWS] * fac - cur[:HEAD_ROWS],
                                     wsum[HEAD_ROWS:] * inv_w - cur[HEAD_ROWS:]], axis=0)
            else:
                d = wsum * inv_w - cur
            ys.append(_dot(d.astype(BF16), poolw[g]))
        y_pool_parts.append(jnp.concatenate(ys, axis=1))
    y_pool = jnp.concatenate(y_pool_parts, axis=0) if n_seq > 1 else y_pool_parts[0]
    y_pool = (y_pool * poolsc[...] * _silu(z_pool)).astype(BF16)

    sm = _dot(hn, wall[:, OFF_SMALL:OFF_GDN])
    lane = lax.broadcasted_iota(jnp.int32, (1, SMALL_W), 1)
    is_decay = (lane < SSD_H) | ((lane >= SSD_H + GDN_HV) & (lane < SSD_H + 2 * GDN_HV))
    sp = _softplus(sm + sbias[...])
    sg = jax.nn.sigmoid(sm)
    av = sp * jnp.where(is_decay, -jnp.exp(salog[...]), 0.0)
    a_hi, a_mid, a_lo = _split3(av)
    bt = btri[...]
    acum = _dot(bt, a_hi) + _dot(bt, a_mid) + _dot(bt, a_lo)
    lane_lt32 = lane < 32
    pk_ac = _pack3(acum, lane_lt32)
    dt_exp = _dot(_pack3(sp, lane_lt32), e_dt[...])
    acum_exp = _dot(pk_ac, e_dt[...])
    beta128 = _dot(_pack3(sg, lane_lt32), e_beta[...])
    gc128 = _dot(pk_ac, e_g[...])

    ps = _dot(hn, wall[:, OFF_SSD:OFF_SMALL])
    u_ssd, z_ssd = ps[:, :SSD_CONV], ps[:, SSD_CONV:]
    acts = []
    for s in range(n_seq):
        rs = slice(s * seq_len, (s + 1) * seq_len)
        sbuf[s, HEAD_ROWS:, :] = u_ssd[rs]
        full = sbuf[s]
        o_sconv[s] = full[seq_len + HEAD_ROWS - CONV_HIST:, :]
        if prompt:
            sbuf[s, 0:HEAD_ROWS, :] = full[seq_len:, :]
        conv = _causal_conv(full, sconvw[...], 4)[HEAD_ROWS:] + sconvb[...]
        acts.append(_silu(conv))
    xbc = jnp.concatenate(acts, axis=0) if n_seq > 1 else acts[0]
    xc, bm, cm_ = xbc[:, :D_MODEL], xbc[:, D_MODEL:D_MODEL + 256], xbc[:, D_MODEL + 256:]

    q_idx = lax.broadcasted_iota(jnp.int32, (CHUNK, D_MODEL), 0)
    s_idx = lax.broadcasted_iota(jnp.int32, (CHUNK, D_MODEL), 1) & (CHUNK - 1)
    causal = s_idx <= q_idx
    diag_sel = (s_idx == q_idx).astype(F32)
    lane128 = lax.broadcasted_iota(jnp.int32, (1, 128), 1)
    lo_half = lane128 < CHUNK

    st = None
    for c in range(n_chunks):
        r = slice(c * CHUNK, (c + 1) * CHUNK)
        s = c // chunks_per_seq
        if c % chunks_per_seq == 0:
            st = o_ssd[0] if prompt else ssd_s[s]
        ac = acum_exp[r]
        a_row = jnp.sum(ac * diag_sel, axis=0, keepdims=True)
        l_all = jnp.where(causal, jnp.exp(ac - a_row), 0.0)
        bc, cc = bm[r], cm_[r]
        cbt = jnp.concatenate(
            [_dot_nt(cc[:, 128 * g:128 * (g + 1)].astype(BF16),
                     jnp.concatenate([bc[:, 128 * g:128 * (g + 1)]] * 8, axis=0).astype(BF16))
             for g in range(SSD_G)], axis=1)
        m_all = (cbt * l_all).astype(BF16)
        xdt = xc[r] * dt_exp[r]
        y_parts = []
        for j in range(8):
            xp = xdt[:, 128 * j:128 * (j + 1)]
            rhs = jnp.concatenate([jnp.where(lo_half, xp, 0.0), jnp.where(lo_half, 0.0, xp)], axis=0)
            y_parts.append(_dot(m_all[:, 128 * j:128 * (j + 1)], rhs.astype(BF16)))
        y = jnp.concatenate(y_parts, axis=1)
        stb = st.astype(BF16)
        y = y + jnp.exp(ac) * jnp.concatenate(
            [_dot(cc[:, 128 * g:128 * (g + 1)].astype(BF16), stb[:, 512 * g:512 * (g + 1)])
             for g in range(SSD_G)], axis=1)
        a_last = ac[CHUNK - 1:CHUNK, :]
        xd = (xdt * jnp.exp(a_last - ac)).astype(BF16)
        upd = jnp.concatenate(
            [_dot_tn(bc[:, 128 * g:128 * (g + 1)].astype(BF16), xd[:, 512 * g:512 * (g + 1)])
             for g in range(SSD_G)], axis=1)
        st = jnp.exp(a_last) * st + upd
        if (c + 1) % chunks_per_seq == 0:
            o_ssd[s] = st
        y = y + ssd_d[...] * xc[r]
        yz = y * _silu(z_ssd[r])
        yn = jnp.concatenate(
            [yz[:, 512 * g:512 * (g + 1)]
             * lax.rsqrt(jnp.mean(yz[:, 512 * g:512 * (g + 1)] ** 2, axis=-1, keepdims=True) + RMS_EPS)
             for g in range(SSD_G)], axis=1)
        yssd[r, :] = yn * ssdnw[...]
    y_ssd = yssd[...].astype(BF16)

    pg = _dot(hn, wall[:, OFF_GDN:OFF_GATE])
    u_gdn, z_gdn = pg[:, :GDN_CONV], pg[:, GDN_CONV:]
    acts = []
    for s in range(n_seq):
        rs = slice(s * seq_len, (s + 1) * seq_len)
        gbuf[s, HEAD_ROWS:, :] = u_gdn[rs]
        full = gbuf[s]
        o_gconv[s] = full[seq_len + HEAD_ROWS - CONV_HIST:, :]
        if prompt:
            gbuf[s, 0:HEAD_ROWS, :] = full[seq_len:, :]
        acts.append(_silu(_causal_conv(full, gconvw[...], 4)[HEAD_ROWS:]))
    qkv = jnp.concatenate(acts, axis=0) if n_seq > 1 else acts[0]

    def l2n(t):
        return t * lax.rsqrt(jnp.sum(t * t, axis=-1, keepdims=True) + 1e-6)

    qn = [l2n(qkv[:, 128 * j:128 * (j + 1)]) * (GDN_DK ** -0.5) for j in range(4)]
    kn = [l2n(qkv[:, 512 + 128 * j:512 + 128 * (j + 1)]) for j in range(4)]
    vc = qkv[:, 1024:]

    ri = lax.broadcasted_iota(jnp.int32, (128, 128), 0)
    ci = lax.broadcasted_iota(jnp.int32, (128, 128), 1)
    same = (ri >> 6) == (ci >> 6)
    incl = same & ((ci & 63) <= (ri & 63))
    strict = same & ((ci & 63) < (ri & 63))
    eye = (ri == ci).astype(F32)
    q64 = lax.broadcasted_iota(jnp.int32, (CHUNK, 128), 0)
    l64 = lax.broadcasted_iota(jnp.int32, (CHUNK, 128), 1)
    diag_sel2 = ((l64 & 63) == q64).astype(F32)
    lo_half64 = l64 < CHUNK

    gs = [None] * GDN_HV
    for c in range(n_chunks):
        r = slice(c * CHUNK, (c + 1) * CHUNK)
        s = c // chunks_per_seq
        for j in range(4):
            h0, h1 = 2 * j, 2 * j + 1
            if c % chunks_per_seq == 0:
                gs[h0] = o_gdn[0, h0] if prompt else gdn_s[s, h0]
                gs[h1] = o_gdn[0, h1] if prompt else gdn_s[s, h1]
            s0, s1 = gs[h0], gs[h1]
            kj, qj = kn[j][r], qn[j][r]
            g0 = gc128[r, 128 * h0:128 * (h0 + 1)]
            g1 = gc128[r, 128 * h1:128 * (h1 + 1)]
            kst = jnp.concatenate([kj, kj], axis=0)
            qst = jnp.concatenate([qj, qj], axis=0)
            bst = jnp.concatenate([beta128[r, 128 * h0:128 * (h0 + 1)],
                                   beta128[r, 128 * h1:128 * (h1 + 1)]], axis=0)
            gst = jnp.concatenate([g0, g1], axis=0)
            vst = jnp.concatenate([vc[r, 128 * h0:128 * (h0 + 1)], vc[r, 128 * h1:128 * (h1 + 1)]], axis=0)
            g_row = jnp.sum(jnp.where(lo_half64, g0, g1) * diag_sel2, axis=0, keepdims=True)
            dec = jnp.exp(gst - g_row)
            eg = jnp.exp(gst)
            kb = kst * bst
            kstb = kst.astype(BF16)
            lp = jnp.where(strict, _dot_nt(kb.astype(BF16), kstb) * dec, 0.0)
            ap = jnp.where(incl, _dot_nt(qst.astype(BF16), kstb) * dec, 0.0)
            t_inv = _neumann_inverse(-lp, eye)
            uw = _dot(t_inv.astype(BF16), jnp.concatenate([vst * bst, kb * eg], axis=1).astype(BF16))
            u_st, w_st = uw[:, :128], uw[:, 128:]
            qg = qst * eg
            r0 = _dot(jnp.concatenate([w_st[:CHUNK], qg[:CHUNK]], axis=0).astype(BF16), s0.astype(BF16))
            r1 = _dot(jnp.concatenate([w_st[CHUNK:], qg[CHUNK:]], axis=0).astype(BF16), s1.astype(BF16))
            vnew = u_st - jnp.concatenate([r0[:CHUNK], r1[:CHUNK]], axis=0)
            o_st = jnp.concatenate([r0[CHUNK:], r1[CHUNK:]], axis=0) + _dot(ap.astype(BF16), vnew.astype(BF16))
            gl0, gl1 = g0[CHUNK - 1:CHUNK, :], g1[CHUNK - 1:CHUNK, :]
            vn_e = vnew * jnp.exp(jnp.concatenate([gl0 - g0, gl1 - g1], axis=0))
            upd = _dot_tn(kj.astype(BF16),
                          jnp.concatenate([vn_e[:CHUNK], vn_e[CHUNK:]], axis=1).astype(BF16))
            gs[h0] = jnp.exp(gl0) * s0 + upd[:, :128]
            gs[h1] = jnp.exp(gl1) * s1 + upd[:, 128:]
            if (c + 1) % chunks_per_seq == 0:
                o_gdn[s, h0] = gs[h0]
                o_gdn[s, h1] = gs[h1]
            ygdn[r, 128 * h0:128 * (h0 + 1)] = o_st[:CHUNK]
            ygdn[r, 128 * h1:128 * (h1 + 1)] = o_st[CHUNK:]
    og = ygdn[...]
    og = jnp.concatenate(
        [og[:, 128 * h:128 * (h + 1)]
         * lax.rsqrt(jnp.mean(og[:, 128 * h:128 * (h + 1)] ** 2, axis=-1, keepdims=True) + RMS_EPS)
         for h in range(GDN_HV)], axis=1)
    y_gdn = (og * gdnnw[...] * _silu(z_gdn)).astype(BF16)

    gates = jax.nn.sigmoid(_dot(hn, wall[:, OFF_GATE:W_ALL]))
    merged = (gates[:, :D_MODEL] * _dot(y_pool, wbr[0])
              + gates[:, D_MODEL:2 * D_MODEL] * _dot(y_ssd, wbr[1])
              + gates[:, 2 * D_MODEL:] * _dot(y_gdn, wbr[2]))
    xn = x + _dot(merged.astype(BF16), wout[...])
    if final_norm:
        xn = xn * lax.rsqrt(jnp.mean(xn * xn, axis=-1, keepdims=True) + RMS_EPS) * fnw[...]
    xo[...] = xn


def _const_spec(shape):
    nd = len(shape)
    return pl.BlockSpec(shape, lambda i, _nd=nd: (0,) * _nd, pipeline_mode=pl.Buffered(1))


def _run_layer(x2d, states, weights, consts, *, prompt, final_norm):
    tokens = x2d.shape[0]
    if prompt:
        n_seq, seq_len, n_state = 1, PROMPT_BLOCK, 1
    else:
        n_seq, seq_len, n_state = SAMPLE_SEQS, CHUNK, tokens // CHUNK
    tb = n_seq * seq_len
    grid = (tokens // tb,)

    def state_spec(shape):
        nd = len(shape)
        if prompt:
            return pl.BlockSpec((1,) + shape, lambda i, _nd=nd: (0,) * (_nd + 1))
        return pl.BlockSpec((n_seq,) + shape, lambda i, _nd=nd: (i,) + (0,) * _nd)

    state_shapes = [(POOL_HIST, D_MODEL), (CONV_HIST, SSD_CONV), (SSD_N, D_MODEL),
                    (CONV_HIST, GDN_CONV), (GDN_HV, GDN_DK, GDN_DV)]
    in_specs = [pl.BlockSpec((tb, D_MODEL), lambda i: (i, 0))]
    args = [x2d]
    if not prompt:
        in_specs += [state_spec(s) for s in state_shapes]
        args += list(states)
    in_specs += [_const_spec(w.shape) for w in weights] + [_const_spec(c.shape) for c in consts]
    args += list(weights) + list(consts)

    out_shape = [jax.ShapeDtypeStruct((tokens, D_MODEL), F32)]
    out_shape += [jax.ShapeDtypeStruct((n_state,) + s, F32) for s in state_shapes]
    out_specs = [pl.BlockSpec((tb, D_MODEL), lambda i: (i, 0))] + [state_spec(s) for s in state_shapes]

    scratch = [pltpu.VMEM((n_seq, HEAD_ROWS + seq_len, D_MODEL), F32),
               pltpu.VMEM((n_seq, HEAD_ROWS + seq_len, SSD_CONV), F32),
               pltpu.VMEM((n_seq, HEAD_ROWS + seq_len, GDN_CONV), F32),
               pltpu.VMEM((tb, D_MODEL), F32),
               pltpu.VMEM((tb, D_MODEL), F32)]
    kern = functools.partial(_layer_kernel, prompt=prompt, n_seq=n_seq, seq_len=seq_len,
                             final_norm=final_norm)
    return pl.pallas_call(
        kern, grid=grid, in_specs=in_specs, out_specs=out_specs, out_shape=out_shape,
        scratch_shapes=scratch,
        compiler_params=pltpu.CompilerParams(dimension_semantics=("arbitrary",),
                                             vmem_limit_bytes=VMEM_LIMIT_BYTES),
    )(*args)


def _expand_matrix(first_lane, n_heads, width):
    rows = jnp.arange(SMALL_W)
    src = rows % 32
    head = src - first_lane
    valid = (rows < 96) & (head >= 0) & (head < n_heads)
    cols = jnp.arange(n_heads * width) // width
    return (valid[:, None] & (head[:, None] == cols[None, :])).astype(BF16)


def _block_tri(tb):
    i = jnp.arange(tb)
    return ((i[:, None] // CHUNK == i[None, :] // CHUNK) & (i[None, :] <= i[:, None])).astype(BF16)


def _layer_weights(l, norm_w, w_in, pool_w, pool_scale, ssd_conv_w, ssd_conv_b, ssd_dt_bias, ssd_A_log,
                   ssd_D, ssd_norm_w, gdn_conv_w, gdn_dt_bias, gdn_A_log, gdn_norm_w, w_br_pool, w_br_ssd,
                   w_br_gdn, w_out, final_norm_w):
    w = w_in[l]
    zpad = jnp.zeros((D_MODEL, SMALL_W - SSD_H - 2 * GDN_HV), F32)
    wall = jnp.concatenate([
        w[:, 0:2048],
        w[:, 2048:3072], w[:, 4096:4608], w[:, 3072:4096],
        w[:, 4608:4624], w[:, 7696:7712], zpad,
        w[:, 4624:7696],
        w[:, 7712:10784],
    ], axis=1).astype(BF16)
    pad8 = jnp.zeros((GDN_HV,), F32)
    pad96 = jnp.zeros((SMALL_W - SSD_H - 2 * GDN_HV,), F32)
    sbias = jnp.concatenate([ssd_dt_bias[l], pad8, gdn_dt_bias[l], pad96])[None, :]
    salog = jnp.concatenate([ssd_A_log[l], pad8, gdn_A_log[l], pad96])[None, :]
    return [
        norm_w[l][None, :], wall, pool_w[l].astype(BF16), pool_scale[l][None, :],
        ssd_conv_w[l], ssd_conv_b[l][None, :], sbias, salog,
        jnp.repeat(ssd_D[l], SSD_P)[None, :], ssd_norm_w[l][None, :],
        gdn_conv_w[l], jnp.tile(gdn_norm_w[l], GDN_HV)[None, :],
        jnp.stack([w_br_pool[l], w_br_ssd[l], w_br_gdn[l]]).astype(BF16), w_out[l].astype(BF16),
        final_norm_w[None, :],
    ]


def kernel(x_prompt, x_sample, state_pool, state_ssd_conv, state_ssd, state_gdn_conv, state_gdn, norm_w, w_in,
           pool_w, pool_scale, ssd_conv_w, ssd_conv_b, ssd_dt_bias, ssd_A_log, ssd_D, ssd_norm_w, gdn_conv_w,
           gdn_dt_bias, gdn_A_log, gdn_norm_w, w_br_pool, w_br_ssd, w_br_gdn, w_out, final_norm_w):
    bp, seq, _ = x_prompt.shape
    bs, dseq, _ = x_sample.shape
    assert bp == 1 and dseq == CHUNK and seq % PROMPT_BLOCK == 0 and bs % SAMPLE_SEQS == 0
    e_mats = [_expand_matrix(0, SSD_H, SSD_P), _expand_matrix(SSD_H, GDN_HV, GDN_DK),
              _expand_matrix(SSD_H + GDN_HV, GDN_HV, GDN_DK)]
    consts_p = [_block_tri(PROMPT_BLOCK)] + e_mats
    consts_s = [_block_tri(SAMPLE_SEQS * CHUNK)] + e_mats

    hp = x_prompt.reshape(seq, D_MODEL)
    hs = x_sample.reshape(bs * dseq, D_MODEL)
    outs_p, outs_s = [], []
    for l in range(DEPTH):
        weights = _layer_weights(l, norm_w, w_in, pool_w, pool_scale, ssd_conv_w, ssd_conv_b, ssd_dt_bias,
                                 ssd_A_log, ssd_D, ssd_norm_w, gdn_conv_w, gdn_dt_bias, gdn_A_log, gdn_norm_w,
                                 w_br_pool, w_br_ssd, w_br_gdn, w_out, final_norm_w)
        last = l == DEPTH - 1
        rp = _run_layer(hp, None, weights, consts_p, prompt=True, final_norm=last)
        ssd_in = state_ssd[l].transpose(0, 3, 1, 2).reshape(bs, SSD_N, D_MODEL)
        rs = _run_layer(hs, (state_pool[l], state_ssd_conv[l], ssd_in, state_gdn_conv[l], state_gdn[l]),
                        weights, consts_s, prompt=False, final_norm=last)
        hp, hs = rp[0], rs[0]
        outs_p.append(rp[1:])
        outs_s.append(rs[1:])

    def stack(outs, k):
        return jnp.stack([o[k] for o in outs])

    def ssd_out(outs):
        st = stack(outs, 2)
        return st.reshape(st.shape[0], st.shape[1], SSD_N, SSD_H, SSD_P).transpose(0, 1, 3, 4, 2)

    return (hp.reshape(bp, seq, D_MODEL), hs.reshape(bs, dseq, D_MODEL),
            stack(outs_p, 0), stack(outs_s, 0), stack(outs_p, 1), stack(outs_s, 1),
            ssd_out(outs_p), ssd_out(outs_s), stack(outs_p, 3), stack(outs_s, 3),
            stack(outs_p, 4), stack(outs_s, 4))
```

```python
import functools

import jax
import jax.numpy as jnp
from jax import lax
from jax.experimental import pallas as pl
from jax.experimental.pallas import tpu as pltpu

F32 = jnp.float32
BF16 = jnp.bfloat16

D_MODEL = 1024
DEPTH = 4
CHUNK = 64
RMS_EPS = 1e-6
POOL_HIST = 15
CONV_HIST = 3
HEAD_ROWS = 16
SSD_H, SSD_P, SSD_N, SSD_G = 16, 64, 128, 2
SSD_CONV = 1536
GDN_HV, GDN_DK, GDN_DV = 8, 128, 128
GDN_CONV = 2048
SMALL_W = 128

OFF_POOL, OFF_SSD, OFF_SMALL, OFF_GDN, OFF_GATE, W_ALL = 0, 2048, 4608, 4736, 7808, 10880

PROMPT_BLOCK = 256
SAMPLE_SEQS = 2
VMEM_LIMIT_BYTES = 60 * 1024 * 1024


def _dot(a, b):
    return jnp.dot(a, b, preferred_element_type=F32)


def _dot_nt(a, b):
    return lax.dot_general(a, b, (((1,), (1,)), ((), ())), preferred_element_type=F32)


def _dot_tn(a, b):
    return lax.dot_general(a, b, (((0,), (0,)), ((), ())), preferred_element_type=F32)


def _silu(x):
    return x * jax.nn.sigmoid(x)


def _softplus(x):
    return jnp.maximum(x, 0.0) + jnp.log1p(jnp.exp(-jnp.abs(x)))


def _split3(a):
    hi = a.astype(BF16)
    r = a - hi.astype(F32)
    mid = r.astype(BF16)
    lo = (r - mid.astype(F32)).astype(BF16)
    return hi, mid, lo


def _pack3(a, lane_lt32):
    hi, mid, lo = _split3(a)
    z = jnp.zeros_like(a)
    p = (jnp.where(lane_lt32, hi.astype(F32), z)
         + pltpu.roll(jnp.where(lane_lt32, mid.astype(F32), z), 32, 1)
         + pltpu.roll(jnp.where(lane_lt32, lo.astype(F32), z), 64, 1))
    return p.astype(BF16)


def _causal_conv(full, w, k_taps):
    acc = full * w[k_taps - 1:k_taps, :]
    for j in range(1, k_taps):
        acc = acc + pltpu.roll(full, j, 0) * w[k_taps - 1 - j:k_taps - j, :]
    return acc


def _neumann_inverse(xs, eye):
    ps = [eye + x for x in xs]
    xbs = [x.astype(BF16) for x in xs]
    ys = [_dot(xb, xb) for xb in xbs]
    for _ in range(4):
        ybs = [y.astype(BF16) for y in ys]
        rs = [_dot(jnp.concatenate([yb, p.astype(BF16)], axis=0), yb) for yb, p in zip(ybs, ps)]
        ys = [r[:128] for r in rs]
        ps = [p + r[128:] for p, r in zip(ps, rs)]
    return [p + _dot(p.astype(BF16), y.astype(BF16)) for p, y in zip(ps, ys)]


def _layer_kernel(*refs, prompt, n_seq, seq_len, final_norm):
    tb = n_seq * seq_len
    n_chunks = tb // CHUNK
    chunks_per_seq = seq_len // CHUNK
    it = iter(refs)
    x_ref = next(it)
    if not prompt:
        pool_h, sconv_h, ssd_s, gconv_h, gdn_s = (next(it) for _ in range(5))
    (normw, wall, poolw, poolsc, sconvw, sconvb, sbias, salog, ssd_d, ssdnw, gconvw, gdnnw,
     wbr, wout, fnw, btri, e_dt, e_beta, e_g) = (next(it) for _ in range(19))
    xo, o_pool, o_sconv, o_ssd, o_gconv, o_gdn = (next(it) for _ in range(6))
    abuf, sbuf, gbuf, yssd, ygdn = (next(it) for _ in range(5))

    step = pl.program_id(0)
    if prompt:
        @pl.when(step == 0)
        def _init():
            abuf[:, 0:HEAD_ROWS, :] = jnp.zeros((n_seq, HEAD_ROWS, D_MODEL), F32)
            sbuf[:, 0:HEAD_ROWS, :] = jnp.zeros((n_seq, HEAD_ROWS, SSD_CONV), F32)
            gbuf[:, 0:HEAD_ROWS, :] = jnp.zeros((n_seq, HEAD_ROWS, GDN_CONV), F32)
            o_ssd[...] = jnp.zeros(o_ssd.shape, F32)
            o_gdn[...] = jnp.zeros(o_gdn.shape, F32)
    else:
        abuf[:, 0:HEAD_ROWS, :] = jnp.zeros((n_seq, HEAD_ROWS, D_MODEL), F32)
        sbuf[:, 0:HEAD_ROWS, :] = jnp.zeros((n_seq, HEAD_ROWS, SSD_CONV), F32)
        gbuf[:, 0:HEAD_ROWS, :] = jnp.zeros((n_seq, HEAD_ROWS, GDN_CONV), F32)
        abuf[:, HEAD_ROWS - POOL_HIST:HEAD_ROWS, :] = pool_h[...]
        sbuf[:, HEAD_ROWS - CONV_HIST:HEAD_ROWS, :] = sconv_h[...]
        gbuf[:, HEAD_ROWS - CONV_HIST:HEAD_ROWS, :] = gconv_h[...]

    x = x_ref[...]
    hn = (x * lax.rsqrt(jnp.mean(x * x, axis=-1, keepdims=True) + RMS_EPS) * normw[...]).astype(BF16)

    pp = _dot(hn, wall[:, OFF_POOL:OFF_SSD])
    a_in, z_pool = pp[:, :D_MODEL], pp[:, D_MODEL:]
    row16 = lax.broadcasted_iota(jnp.int32, (HEAD_ROWS, 256), 0).astype(F32)
    y_pool_parts = []
    for s in range(n_seq):
        rs = slice(s * seq_len, (s + 1) * seq_len)
        abuf[s, HEAD_ROWS:, :] = a_in[rs]
        full = abuf[s]
        o_pool[s] = full[seq_len + HEAD_ROWS - POOL_HIST:, :]
        if prompt:
            abuf[s, 0:HEAD_ROWS, :] = full[seq_len:, :]
        ys = []
        for g in range(4):
            win = 2 << g
            fg = full[:, 256 * g:256 * (g + 1)]
            acc = fg
            for k in range(g + 1):
                acc = acc + pltpu.roll(acc, 1 << k, 0)
            wsum = acc[HEAD_ROWS:]
            cur = fg[HEAD_ROWS:]
            inv_w = 1.0 / win
            if prompt:
                fac = jnp.where(step == 0, 1.0 / jnp.minimum(row16 + 1.0, float(win)), inv_w)
                d = jnp.concatenate([wsum[:HEAD_ROWS] * fac - cur[:HEAD_ROWS],
                                     wsum[HEAD_ROWS:] * inv_w - cur[HEAD_ROWS:]], axis=0)
            else:
                d = wsum * inv_w - cur
            ys.append(_dot(d.astype(BF16), poolw[g]))
        y_pool_parts.append(jnp.concatenate(ys, axis=1))
    y_pool = jnp.concatenate(y_pool_parts, axis=0) if n_seq > 1 else y_pool_parts[0]
    y_pool = (y_pool * poolsc[...] * _silu(z_pool)).astype(BF16)

    sm = _dot(hn, wall[:, OFF_SMALL:OFF_GDN])
    lane = lax.broadcasted_iota(jnp.int32, (1, SMALL_W), 1)
    is_decay = (lane < SSD_H) | ((lane >= SSD_H + GDN_HV) & (lane < SSD_H + 2 * GDN_HV))
    sp = _softplus(sm + sbias[...])
    sg = jax.nn.sigmoid(sm)
    av = sp * jnp.where(is_decay, -jnp.exp(salog[...]), 0.0)
    a_hi, a_mid, a_lo = _split3(av)
    bt = btri[...]
    acum = _dot(bt, a_hi) + _dot(bt, a_mid) + _dot(bt, a_lo)
    lane_lt32 = lane < 32
    pk_ac = _pack3(acum, lane_lt32)
    dt_exp = _dot(_pack3(sp, lane_lt32), e_dt[...])
    acum_exp = _dot(pk_ac, e_dt[...])
    beta128 = _dot(_pack3(sg, lane_lt32), e_beta[...])
    gc128 = _dot(pk_ac, e_g[...])

    ps = _dot(hn, wall[:, OFF_SSD:OFF_SMALL])
    u_ssd, z_ssd = ps[:, :SSD_CONV], ps[:, SSD_CONV:]
    acts = []
    for s in range(n_seq):
        rs = slice(s * seq_len, (s + 1) * seq_len)
        sbuf[s, HEAD_ROWS:, :] = u_ssd[rs]
        full = sbuf[s]
        o_sconv[s] = full[seq_len + HEAD_ROWS - CONV_HIST:, :]
        if prompt:
            sbuf[s, 0:HEAD_ROWS, :] = full[seq_len:, :]
        conv = _causal_conv(full, sconvw[...], 4)[HEAD_ROWS:] + sconvb[...]
        acts.append(_silu(conv))
    xbc = jnp.concatenate(acts, axis=0) if n_seq > 1 else acts[0]
    xc, bm, cm_ = xbc[:, :D_MODEL], xbc[:, D_MODEL:D_MODEL + 256], xbc[:, D_MODEL + 256:]

    q_idx = lax.broadcasted_iota(jnp.int32, (CHUNK, D_MODEL), 0)
    s_idx = lax.broadcasted_iota(jnp.int32, (CHUNK, D_MODEL), 1) & (CHUNK - 1)
    causal = s_idx <= q_idx
    diag_sel = (s_idx == q_idx).astype(F32)
    lane128 = lax.broadcasted_iota(jnp.int32, (1, 128), 1)
    lo_half = lane128 < CHUNK

    st = None
    for c in range(n_chunks):
        r = slice(c * CHUNK, (c + 1) * CHUNK)
        s = c // chunks_per_seq
        if c % chunks_per_seq == 0:
            st = o_ssd[0] if prompt else ssd_s[s]
        ac = acum_exp[r]
        a_row = jnp.sum(ac * diag_sel, axis=0, keepdims=True)
        l_all = jnp.where(causal, jnp.exp(ac - a_row), 0.0)
        bc, cc = bm[r], cm_[r]
        cbt = jnp.concatenate(
            [_dot_nt(cc[:, 128 * g:128 * (g + 1)].astype(BF16),
                     jnp.concatenate([bc[:, 128 * g:128 * (g + 1)]] * 8, axis=0).astype(BF16))
             for g in range(SSD_G)], axis=1)
        m_all = (cbt * l_all).astype(BF16)
        xdt = xc[r] * dt_exp[r]
        y_parts = []
        for j in range(8):
            xp = xdt[:, 128 * j:128 * (j + 1)]
            rhs = jnp.concatenate([jnp.where(lo_half, xp, 0.0), jnp.where(lo_half, 0.0, xp)], axis=0)
            y_parts.append(_dot(m_all[:, 128 * j:128 * (j + 1)], rhs.astype(BF16)))
        y = jnp.concatenate(y_parts, axis=1)
        stb = st.astype(BF16)
        y = y + jnp.exp(ac) * jnp.concatenate(
            [_dot(cc[:, 128 * g:128 * (g + 1)].astype(BF16), stb[:, 512 * g:512 * (g + 1)])
             for g in range(SSD_G)], axis=1)
        a_last = ac[CHUNK - 1:CHUNK, :]
        xd = (xdt * jnp.exp(a_last - ac)).astype(BF16)
        upd = jnp.concatenate(
            [_dot_tn(bc[:, 128 * g:128 * (g + 1)].astype(BF16), xd[:, 512 * g:512 * (g + 1)])
             for g in range(SSD_G)], axis=1)
        st = jnp.exp(a_last) * st + upd
        if (c + 1) % chunks_per_seq == 0:
            o_ssd[s] = st
        y = y + ssd_d[...] * xc[r]
        yz = y * _silu(z_ssd[r])
        yn = jnp.concatenate(
            [yz[:, 512 * g:512 * (g + 1)]
             * lax.rsqrt(jnp.mean(yz[:, 512 * g:512 * (g + 1)] ** 2, axis=-1, keepdims=True) + RMS_EPS)
             for g in range(SSD_G)], axis=1)
        yssd[r, :] = yn * ssdnw[...]
    y_ssd = yssd[...].astype(BF16)

    pg = _dot(hn, wall[:, OFF_GDN:OFF_GATE])
    u_gdn, z_gdn = pg[:, :GDN_CONV], pg[:, GDN_CONV:]
    acts = []
    for s in range(n_seq):
        rs = slice(s * seq_len, (s + 1) * seq_len)
        gbuf[s, HEAD_ROWS:, :] = u_gdn[rs]
        full = gbuf[s]
        o_gconv[s] = full[seq_len + HEAD_ROWS - CONV_HIST:, :]
        if prompt:
            gbuf[s, 0:HEAD_ROWS, :] = full[seq_len:, :]
        acts.append(_silu(_causal_conv(full, gconvw[...], 4)[HEAD_ROWS:]))
    qkv = jnp.concatenate(acts, axis=0) if n_seq > 1 else acts[0]

    def l2n(t):
        return t * lax.rsqrt(jnp.sum(t * t, axis=-1, keepdims=True) + 1e-6)

    qn = [l2n(qkv[:, 128 * j:128 * (j + 1)]) * (GDN_DK ** -0.5) for j in range(4)]
    kn = [l2n(qkv[:, 512 + 128 * j:512 + 128 * (j + 1)]) for j in range(4)]
    vc = qkv[:, 1024:]

    ri = lax.broadcasted_iota(jnp.int32, (128, 128), 0)
    ci = lax.broadcasted_iota(jnp.int32, (128, 128), 1)
    same = (ri >> 6) == (ci >> 6)
    incl = same & ((ci & 63) <= (ri & 63))
    strict = same & ((ci & 63) < (ri & 63))
    eye = (ri == ci).astype(F32)
    q64 = lax.broadcasted_iota(jnp.int32, (CHUNK, 128), 0)
    l64 = lax.broadcasted_iota(jnp.int32, (CHUNK, 128), 1)
    diag_sel2 = ((l64 & 63) == q64).astype(F32)
    lo_half64 = l64 < CHUNK

    combos = [(c, j) for c in range(n_chunks) for j in range(4)]
    pre = {}
    lps = []
    for c, j in combos:
        r = slice(c * CHUNK, (c + 1) * CHUNK)
        h0, h1 = 2 * j, 2 * j + 1
        kj, qj = kn[j][r], qn[j][r]
        g0 = gc128[r, 128 * h0:128 * (h0 + 1)]
        g1 = gc128[r, 128 * h1:128 * (h1 + 1)]
        kst = jnp.concatenate([kj, kj], axis=0)
        qst = jnp.concatenate([qj, qj], axis=0)
        bst = jnp.concatenate([beta128[r, 128 * h0:128 * (h0 + 1)],
                               beta128[r, 128 * h1:128 * (h1 + 1)]], axis=0)
        gst = jnp.concatenate([g0, g1], axis=0)
        vst = jnp.concatenate([vc[r, 128 * h0:128 * (h0 + 1)], vc[r, 128 * h1:128 * (h1 + 1)]], axis=0)
        g_row = jnp.sum(jnp.where(lo_half64, g0, g1) * diag_sel2, axis=0, keepdims=True)
        dec = jnp.exp(gst - g_row)
        eg = jnp.exp(gst)
        kb = kst * bst
        kstb = kst.astype(BF16)
        lps.append(jnp.where(strict, _dot_nt(kb.astype(BF16), kstb) * dec, 0.0))
        gl0, gl1 = g0[CHUNK - 1:CHUNK, :], g1[CHUNK - 1:CHUNK, :]
        pre[(c, j)] = dict(
            ap=jnp.where(incl, _dot_nt(qst.astype(BF16), kstb) * dec, 0.0).astype(BF16),
            rhs=jnp.concatenate([vst * bst, kb * eg], axis=1).astype(BF16),
            qg=(qst * eg).astype(BF16),
            kjb=kj.astype(BF16),
            e_st=jnp.exp(jnp.concatenate([gl0 - g0, gl1 - g1], axis=0)),
            dl0=jnp.exp(gl0), dl1=jnp.exp(gl1))
    t_invs = _neumann_inverse([-lp for lp in lps], eye)
    for (c, j), t_inv in zip(combos, t_invs):
        uw = _dot(t_inv.astype(BF16), pre[(c, j)]["rhs"])
        pre[(c, j)]["u"] = uw[:, :128]
        pre[(c, j)]["w"] = uw[:, 128:].astype(BF16)

    gs = [None] * GDN_HV
    for c in range(n_chunks):
        r = slice(c * CHUNK, (c + 1) * CHUNK)
        s = c // chunks_per_seq
        if c % chunks_per_seq == 0:
            for h in range(GDN_HV):
                gs[h] = o_gdn[0, h] if prompt else gdn_s[s, h]
        rr = []
        for j in range(4):
            p_ = pre[(c, j)]
            for hh in range(2):
                half = slice(hh * CHUNK, (hh + 1) * CHUNK)
                rr.append(_dot(jnp.concatenate([p_["w"][half], p_["qg"][half]], axis=0),
                               gs[2 * j + hh].astype(BF16)))
        vnews = [pre[(c, j)]["u"] - jnp.concatenate([rr[2 * j][:CHUNK], rr[2 * j + 1][:CHUNK]], axis=0)
                 for j in range(4)]
        upds = []
        for j in range(4):
            vn_e = vnews[j] * pre[(c, j)]["e_st"]
            upds.append(_dot_tn(pre[(c, j)]["kjb"],
                                jnp.concatenate([vn_e[:CHUNK], vn_e[CHUNK:]], axis=1).astype(BF16)))
        for j in range(4):
            h0, h1 = 2 * j, 2 * j + 1
            gs[h0] = pre[(c, j)]["dl0"] * gs[h0] + upds[j][:, :128]
            gs[h1] = pre[(c, j)]["dl1"] * gs[h1] + upds[j][:, 128:]
        for j in range(4):
            h0, h1 = 2 * j, 2 * j + 1
            o_st = (jnp.concatenate([rr[2 * j][CHUNK:], rr[2 * j + 1][CHUNK:]], axis=0)
                    + _dot(pre[(c, j)]["ap"], vnews[j].astype(BF16)))
            ygdn[r, 128 * h0:128 * (h0 + 1)] = o_st[:CHUNK]
            ygdn[r, 128 * h1:128 * (h1 + 1)] = o_st[CHUNK:]
        if (c + 1) % chunks_per_seq == 0:
            for h in range(GDN_HV):
                o_gdn[s, h] = gs[h]
    og = ygdn[...]
    og = jnp.concatenate(
        [og[:, 128 * h:128 * (h + 1)]
         * lax.rsqrt(jnp.mean(og[:, 128 * h:128 * (h + 1)] ** 2, axis=-1, keepdims=True) + RMS_EPS)
         for h in range(GDN_HV)], axis=1)
    y_gdn = (og * gdnnw[...] * _silu(z_gdn)).astype(BF16)

    gates = jax.nn.sigmoid(_dot(hn, wall[:, OFF_GATE:W_ALL]))
    merged = (gates[:, :D_MODEL] * _dot(y_pool, wbr[0])
              + gates[:, D_MODEL:2 * D_MODEL] * _dot(y_ssd, wbr[1])
              + gates[:, 2 * D_MODEL:] * _dot(y_gdn, wbr[2]))
    xn = x + _dot(merged.astype(BF16), wout[...])
    if final_norm:
        xn = xn * lax.rsqrt(jnp.mean(xn * xn, axis=-1, keepdims=True) + RMS_EPS) * fnw[...]
    xo[...] = xn


def _const_spec(shape):
    nd = len(shape)
    return pl.BlockSpec(shape, lambda i, _nd=nd: (0,) * _nd, pipeline_mode=pl.Buffered(1))


def _run_layer(x2d, states, weights, consts, *, prompt, final_norm):
    tokens = x2d.shape[0]
    if prompt:
        n_seq, seq_len, n_state = 1, PROMPT_BLOCK, 1
    else:
        n_seq, seq_len, n_state = SAMPLE_SEQS, CHUNK, tokens // CHUNK
    tb = n_seq * seq_len
    grid = (tokens // tb,)

    def state_spec(shape):
        nd = len(shape)
        if prompt:
            return pl.BlockSpec((1,) + shape, lambda i, _nd=nd: (0,) * (_nd + 1))
        return pl.BlockSpec((n_seq,) + shape, lambda i, _nd=nd: (i,) + (0,) * _nd)

    state_shapes = [(POOL_HIST, D_MODEL), (CONV_HIST, SSD_CONV), (SSD_N, D_MODEL),
                    (CONV_HIST, GDN_CONV), (GDN_HV, GDN_DK, GDN_DV)]
    in_specs = [pl.BlockSpec((tb, D_MODEL), lambda i: (i, 0))]
    args = [x2d]
    if not prompt:
        in_specs += [state_spec(s) for s in state_shapes]
        args += list(states)
    in_specs += [_const_spec(w.shape) for w in weights] + [_const_spec(c.shape) for c in consts]
    args += list(weights) + list(consts)

    out_shape = [jax.ShapeDtypeStruct((tokens, D_MODEL), F32)]
    out_shape += [jax.ShapeDtypeStruct((n_state,) + s, F32) for s in state_shapes]
    out_specs = [pl.BlockSpec((tb, D_MODEL), lambda i: (i, 0))] + [state_spec(s) for s in state_shapes]

    scratch = [pltpu.VMEM((n_seq, HEAD_ROWS + seq_len, D_MODEL), F32),
               pltpu.VMEM((n_seq, HEAD_ROWS + seq_len, SSD_CONV), F32),
               pltpu.VMEM((n_seq, HEAD_ROWS + seq_len, GDN_CONV), F32),
               pltpu.VMEM((tb, D_MODEL), F32),
               pltpu.VMEM((tb, D_MODEL), F32)]
    kern = functools.partial(_layer_kernel, prompt=prompt, n_seq=n_seq, seq_len=seq_len,
                             final_norm=final_norm)
    return pl.pallas_call(
        kern, grid=grid, in_specs=in_specs, out_specs=out_specs, out_shape=out_shape,
        scratch_shapes=scratch,
        compiler_params=pltpu.CompilerParams(dimension_semantics=("arbitrary",),
                                             vmem_limit_bytes=VMEM_LIMIT_BYTES),
    )(*args)


def _expand_matrix(first_lane, n_heads, width):
    rows = jnp.arange(SMALL_W)
    src = rows % 32
    head = src - first_lane
    valid = (rows < 96) & (head >= 0) & (head < n_heads)
    cols = jnp.arange(n_heads * width) // width
    return (valid[:, None] & (head[:, None] == cols[None, :])).astype(BF16)


def _block_tri(tb):
    i = jnp.arange(tb)
    return ((i[:, None] // CHUNK == i[None, :] // CHUNK) & (i[None, :] <= i[:, None])).astype(BF16)


def _layer_weights(l, norm_w, w_in, pool_w, pool_scale, ssd_conv_w, ssd_conv_b, ssd_dt_bias, ssd_A_log,
                   ssd_D, ssd_norm_w, gdn_conv_w, gdn_dt_bias, gdn_A_log, gdn_norm_w, w_br_pool, w_br_ssd,
                   w_br_gdn, w_out, final_norm_w):
    w = w_in[l]
    zpad = jnp.zeros((D_MODEL, SMALL_W - SSD_H - 2 * GDN_HV), F32)
    wall = jnp.concatenate([
        w[:, 0:2048],
        w[:, 2048:3072], w[:, 4096:4608], w[:, 3072:4096],
        w[:, 4608:4624], w[:, 7696:7712], zpad,
        w[:, 4624:7696],
        w[:, 7712:10784],
    ], axis=1).astype(BF16)
    pad8 = jnp.zeros((GDN_HV,), F32)
    pad96 = jnp.zeros((SMALL_W - SSD_H - 2 * GDN_HV,), F32)
    sbias = jnp.concatenate([ssd_dt_bias[l], pad8, gdn_dt_bias[l], pad96])[None, :]
    salog = jnp.concatenate([ssd_A_log[l], pad8, gdn_A_log[l], pad96])[None, :]
    return [
        norm_w[l][None, :], wall, pool_w[l].astype(BF16), pool_scale[l][None, :],
        ssd_conv_w[l], ssd_conv_b[l][None, :], sbias, salog,
        jnp.repeat(ssd_D[l], SSD_P)[None, :], ssd_norm_w[l][None, :],
        gdn_conv_w[l], jnp.tile(gdn_norm_w[l], GDN_HV)[None, :],
        jnp.stack([w_br_pool[l], w_br_ssd[l], w_br_gdn[l]]).astype(BF16), w_out[l].astype(BF16),
        final_norm_w[None, :],
    ]


def kernel(x_prompt, x_sample, state_pool, state_ssd_conv, state_ssd, state_gdn_conv, state_gdn, norm_w, w_in,
           pool_w, pool_scale, ssd_conv_w, ssd_conv_b, ssd_dt_bias, ssd_A_log, ssd_D, ssd_norm_w, gdn_conv_w,
           gdn_dt_bias, gdn_A_log, gdn_norm_w, w_br_pool, w_br_ssd, w_br_gdn, w_out, final_norm_w):
    bp, seq, _ = x_prompt.shape
    bs, dseq, _ = x_sample.shape
    assert bp == 1 and dseq == CHUNK and seq % PROMPT_BLOCK == 0 and bs % SAMPLE_SEQS == 0
    e_mats = [_expand_matrix(0, SSD_H, SSD_P), _expand_matrix(SSD_H, GDN_HV, GDN_DK),
              _expand_matrix(SSD_H + GDN_HV, GDN_HV, GDN_DK)]
    consts_p = [_block_tri(PROMPT_BLOCK)] + e_mats
    consts_s = [_block_tri(SAMPLE_SEQS * CHUNK)] + e_mats

    hp = x_prompt.reshape(seq, D_MODEL)
    hs = x_sample.reshape(bs * dseq, D_MODEL)
    outs_p, outs_s = [], []
    for l in range(DEPTH):
        weights = _layer_weights(l, norm_w, w_in, pool_w, pool_scale, ssd_conv_w, ssd_conv_b, ssd_dt_bias,
                                 ssd_A_log, ssd_D, ssd_norm_w, gdn_conv_w, gdn_dt_bias, gdn_A_log, gdn_norm_w,
                                 w_br_pool, w_br_ssd, w_br_gdn, w_out, final_norm_w)
        last = l == DEPTH - 1
        rp = _run_layer(hp, None, weights, consts_p, prompt=True, final_norm=last)
        ssd_in = state_ssd[l].transpose(0, 3, 1, 2).reshape(bs, SSD_N, D_MODEL)
        rs = _run_layer(hs, (state_pool[l], state_ssd_conv[l], ssd_in, state_gdn_conv[l], state_gdn[l]),
                        weights, consts_s, prompt=False, final_norm=last)
        hp, hs = rp[0], rs[0]
        outs_p.append(rp[1:])
        outs_s.append(rs[1:])

    def stack(outs, k):
        return jnp.stack([o[k] for o in outs])

    def ssd_out(outs):
        st = stack(outs, 2)
        return st.reshape(st.shape[0], st.shape[1], SSD_N, SSD_H, SSD_P).transpose(0, 1, 3, 4, 2)

    return (hp.reshape(bp, seq, D_MODEL), hs.reshape(bs, dseq, D_MODEL),
            stack(outs_p, 0), stack(outs_s, 0), stack(outs_p, 1), stack(outs_s, 1),
            ssd_out(outs_p), ssd_out(outs_s), stack(outs_p, 3), stack(outs_s, 3),
            stack(outs_p, 4), stack(outs_s, 4))
```

```python
import functools

import jax
import jax.numpy as jnp
from jax import lax
from jax.experimental import pallas as pl
from jax.experimental.pallas import tpu as pltpu

F32 = jnp.float32
BF16 = jnp.bfloat16

D_MODEL = 1024
DEPTH = 4
CHUNK = 64
RMS_EPS = 1e-6
POOL_HIST = 15
CONV_HIST = 3
HEAD_ROWS = 16
SSD_H, SSD_P, SSD_N, SSD_G = 16, 64, 128, 2
SSD_CONV = 1536
GDN_HV, GDN_DK, GDN_DV = 8, 128, 128
GDN_CONV = 2048
SMALL_W = 128

OFF_POOL, OFF_SSD, OFF_SMALL, OFF_GDN, OFF_GATE, W_ALL = 0, 2048, 4608, 4736, 7808, 10880

PROMPT_BLOCK = 256
PROMPT_SUB_BLOCKS = 1
SAMPLE_SEQS = 2
VMEM_LIMIT_BYTES = 60 * 1024 * 1024

STATE_SHAPES = ((POOL_HIST, D_MODEL), (CONV_HIST, SSD_CONV), (D_MODEL, SSD_N),
                (CONV_HIST, GDN_CONV), (GDN_HV, GDN_DK, GDN_DV))


def _dot(a, b):
    return jnp.dot(a, b, preferred_element_type=F32)


def _dot_nt(a, b):
    return lax.dot_general(a, b, (((1,), (1,)), ((), ())), preferred_element_type=F32)


def _dot_tn(a, b):
    return lax.dot_general(a, b, (((0,), (0,)), ((), ())), preferred_element_type=F32)


def _silu(x):
    return x * jax.nn.sigmoid(x)


def _softplus(x):
    return jnp.maximum(x, 0.0) + jnp.log1p(jnp.exp(-jnp.abs(x)))


def _split3(a):
    hi = a.astype(BF16)
    r = a - hi.astype(F32)
    mid = r.astype(BF16)
    lo = (r - mid.astype(F32)).astype(BF16)
    return hi, mid, lo


def _pack3(a, lane_lt32):
    hi, mid, lo = _split3(a)
    z = jnp.zeros_like(a)
    p = (jnp.where(lane_lt32, hi.astype(F32), z)
         + pltpu.roll(jnp.where(lane_lt32, mid.astype(F32), z), 32, 1)
         + pltpu.roll(jnp.where(lane_lt32, lo.astype(F32), z), 64, 1))
    return p.astype(BF16)


def _causal_conv(buf, s, row, length, w, k_taps):
    acc = buf[s, row:row + length, :] * w[k_taps - 1:k_taps, :]
    for j in range(1, k_taps):
        acc = acc + buf[s, row - j:row - j + length, :] * w[k_taps - 1 - j:k_taps - j, :]
    return acc


def _neumann_inverse(xs, eye):
    ps = [eye + x for x in xs]
    xbs = [x.astype(BF16) for x in xs]
    ys = [_dot(xb, xb) for xb in xbs]
    for _ in range(4):
        ybs = [y.astype(BF16) for y in ys]
        rs = [_dot(jnp.concatenate([yb, p.astype(BF16)], axis=0), yb) for yb, p in zip(ybs, ps)]
        ys = [r[:128] for r in rs]
        ps = [p + r[128:] for p, r in zip(ps, rs)]
    return [p + _dot(p.astype(BF16), y.astype(BF16)) for p, y in zip(ps, ys)]


def _layer_kernel(*refs, prompt, n_seq, seq_len, n_sub, final_norm, n_alias):
    tb = n_seq * seq_len
    tbs = tb // n_sub
    n_chunks = tbs // CHUNK
    it = iter(refs)
    x_ref = next(it)
    if not prompt:
        pool_h, sconv_h, ssd_s, gconv_h, gdn_s = (next(it) for _ in range(5))
    (normw, wall, poolw, poolsc, sconvw, sconvb, sbias, salog, ssd_d, ssdnw, gconvw, gdnnw,
     wbr, wout, fnw, btri, e_dt, e_beta, e_g) = (next(it) for _ in range(19))
    for _ in range(n_alias):
        next(it)
    xo, o_pool, o_sconv, o_ssd, o_gconv, o_gdn = (next(it) for _ in range(6))
    abuf, sbuf, gbuf, yssd, ygdn, st_scr, hn_scr = (next(it) for _ in range(7))

    step = pl.program_id(0)
    if prompt:
        @pl.when(step == 0)
        def _init():
            abuf[:, 0:HEAD_ROWS, :] = jnp.zeros((n_seq, HEAD_ROWS, D_MODEL), F32)
            sbuf[:, 0:HEAD_ROWS, :] = jnp.zeros((n_seq, HEAD_ROWS, SSD_CONV), F32)
            gbuf[:, 0:HEAD_ROWS, :] = jnp.zeros((n_seq, HEAD_ROWS, GDN_CONV), F32)
            st_scr[...] = jnp.zeros(st_scr.shape, F32)
            o_gdn[...] = jnp.zeros(o_gdn.shape, F32)
    else:
        abuf[:, 0:HEAD_ROWS, :] = jnp.zeros((n_seq, HEAD_ROWS, D_MODEL), F32)
        sbuf[:, 0:HEAD_ROWS, :] = jnp.zeros((n_seq, HEAD_ROWS, SSD_CONV), F32)
        gbuf[:, 0:HEAD_ROWS, :] = jnp.zeros((n_seq, HEAD_ROWS, GDN_CONV), F32)
        abuf[:, HEAD_ROWS - POOL_HIST:HEAD_ROWS, :] = pool_h[...]
        sbuf[:, HEAD_ROWS - CONV_HIST:HEAD_ROWS, :] = sconv_h[...]
        gbuf[:, HEAD_ROWS - CONV_HIST:HEAD_ROWS, :] = gconv_h[...]

    row16 = lax.broadcasted_iota(jnp.int32, (HEAD_ROWS, 256), 0).astype(F32)
    lane = lax.broadcasted_iota(jnp.int32, (1, SMALL_W), 1)
    is_decay = (lane < SSD_H) | ((lane >= SSD_H + GDN_HV) & (lane < SSD_H + 2 * GDN_HV))
    lane_lt32 = lane < 32
    q_idx = lax.broadcasted_iota(jnp.int32, (CHUNK, D_MODEL), 0)
    s_idx = lax.broadcasted_iota(jnp.int32, (CHUNK, D_MODEL), 1) & (CHUNK - 1)
    causal = s_idx <= q_idx
    diag_sel = (s_idx == q_idx).astype(F32)
    ri = lax.broadcasted_iota(jnp.int32, (128, 128), 0)
    ci = lax.broadcasted_iota(jnp.int32, (128, 128), 1)
    same = (ri >> 6) == (ci >> 6)
    incl = same & ((ci & 63) <= (ri & 63))
    strict = same & ((ci & 63) < (ri & 63))
    eye = (ri == ci).astype(F32)
    q64 = lax.broadcasted_iota(jnp.int32, (CHUNK, 128), 0)
    l64 = lax.broadcasted_iota(jnp.int32, (CHUNK, 128), 1)
    diag_sel2 = ((l64 & 63) == q64).astype(F32)
    lo_half64 = l64 < CHUNK

    def l2n(t):
        return t * lax.rsqrt(jnp.sum(t * t, axis=-1, keepdims=True) + 1e-6)

    def sub_block(sub):
        r0 = sub * tbs
        rows = slice(r0, r0 + tbs)
        if prompt:
            seqs = [(0, r0, 0, tbs)]
        else:
            seqs = [(s, 0, s * seq_len, seq_len) for s in range(n_seq)]
        chunks_per_seq = seqs[0][3] // CHUNK

        x = x_ref[rows, :]
        hn_scr[rows, :] = (x * lax.rsqrt(jnp.mean(x * x, axis=-1, keepdims=True) + RMS_EPS)
                           * normw[...]).astype(BF16)
        pp = _dot(hn_scr[rows, :], wall[:, OFF_POOL:OFF_SSD])
        sm = _dot(hn_scr[rows, :], wall[:, OFF_SMALL:OFF_GDN])

        a_in, z_pool = pp[:, :D_MODEL], pp[:, D_MODEL:]
        y_pool_parts = []
        for s, b0, x0, ln in seqs:
            abuf[s, b0 + HEAD_ROWS:b0 + HEAD_ROWS + ln, :] = a_in[x0:x0 + ln]
            full = abuf[s, b0:b0 + HEAD_ROWS + ln, :]
            o_pool[s] = full[ln + HEAD_ROWS - POOL_HIST:, :]
            ys = []
            for g in range(4):
                win = 2 << g
                fg = full[:, 256 * g:256 * (g + 1)]
                acc = fg
                for k in range(g + 1):
                    acc = acc + pltpu.roll(acc, 1 << k, 0)
                wsum = acc[HEAD_ROWS:]
                cur = fg[HEAD_ROWS:]
                inv_w = 1.0 / win
                if prompt and sub == 0:
                    fac = jnp.where(step == 0, 1.0 / jnp.minimum(row16 + 1.0, float(win)), inv_w)
                    d = jnp.concatenate([wsum[:HEAD_ROWS] * fac - cur[:HEAD_ROWS],
                                         wsum[HEAD_ROWS:] * inv_w - cur[HEAD_ROWS:]], axis=0)
                else:
                    d = wsum * inv_w - cur
                ys.append(_dot(d.astype(BF16), poolw[g]))
            y_pool_parts.append(jnp.concatenate(ys, axis=1))
        y_pool = jnp.concatenate(y_pool_parts, axis=0) if len(seqs) > 1 else y_pool_parts[0]
        y_pool = (y_pool * poolsc[...] * _silu(z_pool)).astype(BF16)

        sp = _softplus(sm + sbias[...])
        sg = jax.nn.sigmoid(sm)
        av = sp * jnp.where(is_decay, -jnp.exp(salog[...]), 0.0)
        a_hi, a_mid, a_lo = _split3(av)
        bt = btri[...]
        acum = _dot(bt, a_hi) + _dot(bt, a_mid) + _dot(bt, a_lo)
        pk_ac = _pack3(acum, lane_lt32)
        dt_exp = _dot(_pack3(sp, lane_lt32), e_dt[...])
        acum_exp = _dot(pk_ac, e_dt[...])
        beta128 = _dot(_pack3(sg, lane_lt32), e_beta[...])
        gc128 = _dot(pk_ac, e_g[...])

        ps = _dot(hn_scr[rows, :], wall[:, OFF_SSD:OFF_SMALL])
        u_ssd, z_ssd = ps[:, :SSD_CONV], ps[:, SSD_CONV:]
        acts = []
        for s, b0, x0, ln in seqs:
            sbuf[s, b0 + HEAD_ROWS:b0 + HEAD_ROWS + ln, :] = u_ssd[x0:x0 + ln]
            o_sconv[s] = sbuf[s, b0 + ln + HEAD_ROWS - CONV_HIST:b0 + ln + HEAD_ROWS, :]
            acts.append(_silu(_causal_conv(sbuf, s, b0 + HEAD_ROWS, ln, sconvw[...], 4) + sconvb[...]))
        xbc = jnp.concatenate(acts, axis=0) if len(seqs) > 1 else acts[0]
        xc, bm, cm_ = xbc[:, :D_MODEL], xbc[:, D_MODEL:D_MODEL + 256], xbc[:, D_MODEL + 256:]

        st = None
        for c in range(n_chunks):
            r = slice(c * CHUNK, (c + 1) * CHUNK)
            s = c // chunks_per_seq
            if c % chunks_per_seq == 0:
                st = st_scr[...] if prompt else ssd_s[s].T
            ac = acum_exp[r]
            a_row = jnp.sum(ac * diag_sel, axis=0, keepdims=True)
            l_all = jnp.where(causal, jnp.exp(ac - a_row), 0.0)
            bc, cc = bm[r], cm_[r]
            cbt = jnp.concatenate(
                [_dot_nt(cc[:, 128 * g:128 * (g + 1)].astype(BF16),
                         jnp.concatenate([bc[:, 128 * g:128 * (g + 1)].astype(BF16)] * 8, axis=0))
                 for g in range(SSD_G)], axis=1)
            m_all = (cbt * l_all).astype(BF16)
            xdt = xc[r] * dt_exp[r]
            xdtb = xdt.astype(BF16)
            zb = jnp.zeros((CHUNK, 128), BF16)
            y_parts = []
            for j in range(8):
                xp = xdtb[:, 128 * j:128 * (j + 1)]
                rhs = jnp.concatenate([jnp.where(lo_half64, xp, zb), jnp.where(lo_half64, zb, xp)], axis=0)
                y_parts.append(_dot(m_all[:, 128 * j:128 * (j + 1)], rhs))
            y = jnp.concatenate(y_parts, axis=1)
            stb = st.astype(BF16)
            y = y + jnp.exp(ac) * jnp.concatenate(
                [_dot(cc[:, 128 * g:128 * (g + 1)].astype(BF16), stb[:, 512 * g:512 * (g + 1)])
                 for g in range(SSD_G)], axis=1)
            a_last = ac[CHUNK - 1:CHUNK, :]
            xd = (xdt * jnp.exp(a_last - ac)).astype(BF16)
            upd = jnp.concatenate(
                [_dot_tn(bc[:, 128 * g:128 * (g + 1)].astype(BF16), xd[:, 512 * g:512 * (g + 1)])
                 for g in range(SSD_G)], axis=1)
            st = jnp.exp(a_last) * st + upd
            if (c + 1) % chunks_per_seq == 0:
                if prompt:
                    st_scr[...] = st
                else:
                    o_ssd[s] = st.T
            y = y + ssd_d[...] * xc[r]
            yz = y * _silu(z_ssd[r])
            yn = jnp.concatenate(
                [yz[:, 512 * g:512 * (g + 1)]
                 * lax.rsqrt(jnp.mean(yz[:, 512 * g:512 * (g + 1)] ** 2, axis=-1, keepdims=True) + RMS_EPS)
                 for g in range(SSD_G)], axis=1)
            yssd[r0 + c * CHUNK:r0 + (c + 1) * CHUNK, :] = yn * ssdnw[...]
        y_ssd = yssd[rows, :].astype(BF16)

        pg = _dot(hn_scr[rows, :], wall[:, OFF_GDN:OFF_GATE])
        u_gdn, z_gdn = pg[:, :GDN_CONV], pg[:, GDN_CONV:]
        acts = []
        for s, b0, x0, ln in seqs:
            gbuf[s, b0 + HEAD_ROWS:b0 + HEAD_ROWS + ln, :] = u_gdn[x0:x0 + ln]
            o_gconv[s] = gbuf[s, b0 + ln + HEAD_ROWS - CONV_HIST:b0 + ln + HEAD_ROWS, :]
            acts.append(_silu(_causal_conv(gbuf, s, b0 + HEAD_ROWS, ln, gconvw[...], 4)))
        qkv = jnp.concatenate(acts, axis=0) if len(seqs) > 1 else acts[0]

        qn = [l2n(qkv[:, 128 * j:128 * (j + 1)]) * (GDN_DK ** -0.5) for j in range(4)]
        kn = [l2n(qkv[:, 512 + 128 * j:512 + 128 * (j + 1)]) for j in range(4)]
        vc = qkv[:, 1024:]

        combos = [(c, j) for c in range(n_chunks) for j in range(4)]
        pre = {}
        lps = []
        for c, j in combos:
            r = slice(c * CHUNK, (c + 1) * CHUNK)
            h0, h1 = 2 * j, 2 * j + 1
            kj, qj = kn[j][r], qn[j][r]
            g0 = gc128[r, 128 * h0:128 * (h0 + 1)]
            g1 = gc128[r, 128 * h1:128 * (h1 + 1)]
            kst = jnp.concatenate([kj, kj], axis=0)
            qst = jnp.concatenate([qj, qj], axis=0)
            bst = jnp.concatenate([beta128[r, 128 * h0:128 * (h0 + 1)],
                                   beta128[r, 128 * h1:128 * (h1 + 1)]], axis=0)
            gst = jnp.concatenate([g0, g1], axis=0)
            vst = jnp.concatenate([vc[r, 128 * h0:128 * (h0 + 1)], vc[r, 128 * h1:128 * (h1 + 1)]], axis=0)
            g_row = jnp.sum(jnp.where(lo_half64, g0, g1) * diag_sel2, axis=0, keepdims=True)
            dec = jnp.exp(gst - g_row)
            eg = jnp.exp(gst)
            kq = _dot_nt(jnp.concatenate([kj, qj], axis=0).astype(BF16), kst.astype(BF16))
            kk2 = jnp.concatenate([kq[:CHUNK], kq[:CHUNK]], axis=0)
            qk2 = jnp.concatenate([kq[CHUNK:], kq[CHUNK:]], axis=0)
            lps.append(jnp.where(strict, kk2 * bst * dec, 0.0))
            gl0, gl1 = g0[CHUNK - 1:CHUNK, :], g1[CHUNK - 1:CHUNK, :]
            pre[(c, j)] = dict(
                ap=jnp.where(incl, qk2 * dec, 0.0).astype(BF16),
                rhs=jnp.concatenate([vst * bst, kst * bst * eg], axis=1).astype(BF16),
                qg=(qst * eg).astype(BF16),
                kjb=kj.astype(BF16),
                e_st=jnp.exp(jnp.concatenate([gl0 - g0, gl1 - g1], axis=0)),
                dl0=jnp.exp(gl0), dl1=jnp.exp(gl1))
        t_invs = _neumann_inverse([-lp for lp in lps], eye)
        for (c, j), t_inv in zip(combos, t_invs):
            uw = _dot(t_inv.astype(BF16), pre[(c, j)]["rhs"])
            pre[(c, j)]["u"] = uw[:, :128]
            pre[(c, j)]["w"] = uw[:, 128:].astype(BF16)

        gs = [None] * GDN_HV
        for c in range(n_chunks):
            s = c // chunks_per_seq
            if c % chunks_per_seq == 0:
                for h in range(GDN_HV):
                    gs[h] = o_gdn[0, h] if prompt else gdn_s[s, h]
            rr = []
            for j in range(4):
                p_ = pre[(c, j)]
                for hh in range(2):
                    half = slice(hh * CHUNK, (hh + 1) * CHUNK)
                    rr.append(_dot(jnp.concatenate([p_["w"][half], p_["qg"][half]], axis=0),
                                   gs[2 * j + hh].astype(BF16)))
            vnews = [pre[(c, j)]["u"] - jnp.concatenate([rr[2 * j][:CHUNK], rr[2 * j + 1][:CHUNK]], axis=0)
                     for j in range(4)]
            upds = []
            for j in range(4):
                vn_e = vnews[j] * pre[(c, j)]["e_st"]
                upds.append(_dot_tn(pre[(c, j)]["kjb"],
                                    jnp.concatenate([vn_e[:CHUNK], vn_e[CHUNK:]], axis=1).astype(BF16)))
            for j in range(4):
                h0, h1 = 2 * j, 2 * j + 1
                gs[h0] = pre[(c, j)]["dl0"] * gs[h0] + upds[j][:, :128]
                gs[h1] = pre[(c, j)]["dl1"] * gs[h1] + upds[j][:, 128:]
            yr = slice(r0 + c * CHUNK, r0 + (c + 1) * CHUNK)
            for j in range(4):
                h0, h1 = 2 * j, 2 * j + 1
                o_st = (jnp.concatenate([rr[2 * j][CHUNK:], rr[2 * j + 1][CHUNK:]], axis=0)
                        + _dot(pre[(c, j)]["ap"], vnews[j].astype(BF16)))
                ygdn[yr, 128 * h0:128 * (h0 + 1)] = o_st[:CHUNK]
                ygdn[yr, 128 * h1:128 * (h1 + 1)] = o_st[CHUNK:]
            if (c + 1) % chunks_per_seq == 0:
                for h in range(GDN_HV):
                    o_gdn[s, h] = gs[h]
        og = ygdn[rows, :]
        og = jnp.concatenate(
            [og[:, 128 * h:128 * (h + 1)]
             * lax.rsqrt(jnp.mean(og[:, 128 * h:128 * (h + 1)] ** 2, axis=-1, keepdims=True) + RMS_EPS)
             for h in range(GDN_HV)], axis=1)
        y_gdn = (og * gdnnw[...] * _silu(z_gdn)).astype(BF16)

        gates = jax.nn.sigmoid(_dot(hn_scr[rows, :], wall[:, OFF_GATE:W_ALL]))
        merged = (gates[:, :D_MODEL] * _dot(y_pool, wbr[0])
                  + gates[:, D_MODEL:2 * D_MODEL] * _dot(y_ssd, wbr[1])
                  + gates[:, 2 * D_MODEL:] * _dot(y_gdn, wbr[2]))
        xn = x + _dot(merged.astype(BF16), wout[...])
        if final_norm:
            xn = xn * lax.rsqrt(jnp.mean(xn * xn, axis=-1, keepdims=True) + RMS_EPS) * fnw[...]
        xo[rows, :] = xn

    for sub in range(n_sub):
        sub_block(sub)

    if prompt:
        abuf[0, 0:HEAD_ROWS, :] = abuf[0, tb:tb + HEAD_ROWS, :]
        sbuf[0, 0:HEAD_ROWS, :] = sbuf[0, tb:tb + HEAD_ROWS, :]
        gbuf[0, 0:HEAD_ROWS, :] = gbuf[0, tb:tb + HEAD_ROWS, :]

        @pl.when(step == pl.num_programs(0) - 1)
        def _final_state():
            o_ssd[0] = st_scr[...].T


def _layer_spec(arr, layer):
    nd = arr.ndim - 1
    return pl.BlockSpec((None,) + arr.shape[1:], lambda i, _l=layer, _nd=nd: (_l,) + (0,) * _nd,
                        pipeline_mode=pl.Buffered(1))


def _const_spec(arr):
    nd = arr.ndim
    return pl.BlockSpec(arr.shape, lambda i, _nd=nd: (0,) * _nd, pipeline_mode=pl.Buffered(1))


def _run_layer(layer, x2d, states, weights, fnw, consts, stacked, *, prompt, final_norm):
    tokens = x2d.shape[0]
    if prompt:
        n_seq, seq_len, n_state, n_sub = 1, PROMPT_BLOCK, 1, PROMPT_SUB_BLOCKS
    else:
        n_seq, seq_len, n_state, n_sub = SAMPLE_SEQS, CHUNK, tokens // CHUNK, 1
    tb = n_seq * seq_len
    grid = (tokens // tb,)

    def state_spec(shape):
        nd = len(shape)
        if prompt:
            return pl.BlockSpec((None, 1) + shape, lambda i, _nd=nd: (layer, 0) + (0,) * _nd)
        return pl.BlockSpec((None, n_seq) + shape, lambda i, _nd=nd: (layer, i) + (0,) * _nd)

    in_specs = [pl.BlockSpec((tb, D_MODEL), lambda i: (i, 0))]
    args = [x2d]
    if not prompt:
        in_specs += [state_spec(s) for s in STATE_SHAPES]
        args += list(states)
    in_specs += [_layer_spec(w, layer) for w in weights] + [_const_spec(fnw)] + [_const_spec(c) for c in consts]
    args += list(weights) + [fnw] + list(consts)
    aliases = {}
    for k, buf in enumerate(stacked):
        aliases[len(args)] = 1 + k
        in_specs.append(pl.BlockSpec(memory_space=pl.ANY))
        args.append(buf)

    out_shape = [jax.ShapeDtypeStruct((tokens, D_MODEL), F32)]
    out_shape += [jax.ShapeDtypeStruct((DEPTH, n_state) + s, F32) for s in STATE_SHAPES]
    out_specs = [pl.BlockSpec((tb, D_MODEL), lambda i: (i, 0))] + [state_spec(s) for s in STATE_SHAPES]

    scratch = [pltpu.VMEM((n_seq, HEAD_ROWS + seq_len, D_MODEL), F32),
               pltpu.VMEM((n_seq, HEAD_ROWS + seq_len, SSD_CONV), F32),
               pltpu.VMEM((n_seq, HEAD_ROWS + seq_len, GDN_CONV), F32),
               pltpu.VMEM((tb, D_MODEL), F32),
               pltpu.VMEM((tb, D_MODEL), F32),
               pltpu.VMEM((SSD_N, D_MODEL), F32),
               pltpu.VMEM((tb, D_MODEL), BF16)]
    kern = functools.partial(_layer_kernel, prompt=prompt, n_seq=n_seq, seq_len=seq_len, n_sub=n_sub,
                             final_norm=final_norm, n_alias=len(aliases))
    return pl.pallas_call(
        kern, grid=grid, in_specs=in_specs, out_specs=out_specs, out_shape=out_shape,
        scratch_shapes=scratch, input_output_aliases=aliases,
        compiler_params=pltpu.CompilerParams(dimension_semantics=("arbitrary",),
                                             vmem_limit_bytes=VMEM_LIMIT_BYTES),
    )(*args)


def _expand_matrix(first_lane, n_heads, width):
    rows = jnp.arange(SMALL_W)
    src = rows % 32
    head = src - first_lane
    valid = (rows < 96) & (head >= 0) & (head < n_heads)
    cols = jnp.arange(n_heads * width) // width
    return (valid[:, None] & (head[:, None] == cols[None, :])).astype(BF16)


def _block_tri(tb):
    i = jnp.arange(tb)
    return ((i[:, None] // CHUNK == i[None, :] // CHUNK) & (i[None, :] <= i[:, None])).astype(BF16)


def _all_layer_weights(norm_w, w_in, pool_w, pool_scale, ssd_conv_w, ssd_conv_b, ssd_dt_bias, ssd_A_log,
                       ssd_D, ssd_norm_w, gdn_conv_w, gdn_dt_bias, gdn_A_log, gdn_norm_w, w_br_pool,
                       w_br_ssd, w_br_gdn, w_out):
    n_pad = SMALL_W - SSD_H - 2 * GDN_HV
    wall = jnp.concatenate([
        w_in[:, :, 0:2048],
        w_in[:, :, 2048:3072], w_in[:, :, 4096:4608], w_in[:, :, 3072:4096],
        w_in[:, :, 4608:4624], w_in[:, :, 7696:7712], jnp.zeros((DEPTH, D_MODEL, n_pad), F32),
        w_in[:, :, 4624:7696],
        w_in[:, :, 7712:10784],
    ], axis=2).astype(BF16)
    pad8 = jnp.zeros((DEPTH, GDN_HV), F32)
    pad = jnp.zeros((DEPTH, n_pad), F32)
    sbias = jnp.concatenate([ssd_dt_bias, pad8, gdn_dt_bias, pad], axis=1)[:, None, :]
    salog = jnp.concatenate([ssd_A_log, pad8, gdn_A_log, pad], axis=1)[:, None, :]
    return [
        norm_w[:, None, :], wall, pool_w.astype(BF16), pool_scale[:, None, :],
        ssd_conv_w, ssd_conv_b[:, None, :], sbias, salog,
        jnp.repeat(ssd_D, SSD_P, axis=1)[:, None, :], ssd_norm_w[:, None, :],
        gdn_conv_w, jnp.tile(gdn_norm_w, (1, GDN_HV))[:, None, :],
        jnp.stack([w_br_pool, w_br_ssd, w_br_gdn], axis=1).astype(BF16), w_out.astype(BF16),
    ]


def kernel(x_prompt, x_sample, state_pool, state_ssd_conv, state_ssd, state_gdn_conv, state_gdn, norm_w, w_in,
           pool_w, pool_scale, ssd_conv_w, ssd_conv_b, ssd_dt_bias, ssd_A_log, ssd_D, ssd_norm_w, gdn_conv_w,
           gdn_dt_bias, gdn_A_log, gdn_norm_w, w_br_pool, w_br_ssd, w_br_gdn, w_out, final_norm_w):
    bp, seq, _ = x_prompt.shape
    bs, dseq, _ = x_sample.shape
    assert bp == 1 and dseq == CHUNK and seq % PROMPT_BLOCK == 0 and bs % SAMPLE_SEQS == 0
    e_mats = [_expand_matrix(0, SSD_H, SSD_P), _expand_matrix(SSD_H, GDN_HV, GDN_DK),
              _expand_matrix(SSD_H + GDN_HV, GDN_HV, GDN_DK)]
    consts_p = [_block_tri(PROMPT_BLOCK // PROMPT_SUB_BLOCKS)] + e_mats
    consts_s = [_block_tri(SAMPLE_SEQS * CHUNK)] + e_mats
    weights = _all_layer_weights(norm_w, w_in, pool_w, pool_scale, ssd_conv_w, ssd_conv_b, ssd_dt_bias,
                                 ssd_A_log, ssd_D, ssd_norm_w, gdn_conv_w, gdn_dt_bias, gdn_A_log, gdn_norm_w,
                                 w_br_pool, w_br_ssd, w_br_gdn, w_out)
    fnw = final_norm_w[None, :]
    states_s = (state_pool, state_ssd_conv, state_ssd.reshape(DEPTH, bs, D_MODEL, SSD_N),
                state_gdn_conv, state_gdn)

    hp = x_prompt.reshape(seq, D_MODEL)
    hs = x_sample.reshape(bs * dseq, D_MODEL)
    stk_p = [jnp.zeros((DEPTH, 1) + s, F32) for s in STATE_SHAPES]
    stk_s = [jnp.zeros((DEPTH, bs) + s, F32) for s in STATE_SHAPES]
    for l in range(DEPTH):
        last = l == DEPTH - 1
        rp = _run_layer(l, hp, None, weights, fnw, consts_p, stk_p, prompt=True, final_norm=last)
        rs = _run_layer(l, hs, states_s, weights, fnw, consts_s, stk_s, prompt=False, final_norm=last)
        hp, stk_p = rp[0], rp[1:]
        hs, stk_s = rs[0], rs[1:]

    def ssd_out(st):
        return st.reshape(DEPTH, st.shape[1], SSD_H, SSD_P, SSD_N)

    return (hp.reshape(bp, seq, D_MODEL), hs.reshape(bs, dseq, D_MODEL),
            stk_p[0], stk_s[0], stk_p[1], stk_s[1], ssd_out(stk_p[2]), ssd_out(stk_s[2]),
            stk_p[3], stk_s[3], stk_p[4], stk_s[4])
```

```python
import functools

import jax
import jax.numpy as jnp
from jax import lax
from jax.experimental import pallas as pl
from jax.experimental.pallas import tpu as pltpu

F32 = jnp.float32
BF16 = jnp.bfloat16

D_MODEL = 1024
DEPTH = 4
CHUNK = 64
RMS_EPS = 1e-6
POOL_HIST = 15
CONV_HIST = 3
HEAD_ROWS = 16
SSD_H, SSD_P, SSD_N, SSD_G = 16, 64, 128, 2
SSD_CONV = 1536
GDN_HV, GDN_DK, GDN_DV = 8, 128, 128
GDN_CONV = 2048
SMALL_W = 128

W_POOL, W_SSD = 2048, 2560
IN_DT, IN_GDN, IN_BETA, IN_GATE, IN_END = 4608, 4624, 7696, 7712, 10784

PROMPT_BLOCK = 256
PROMPT_SUB_BLOCKS = 1
SAMPLE_SEQS = 2
VMEM_LIMIT_BYTES = 60 * 1024 * 1024

STATE_SHAPES = ((POOL_HIST, D_MODEL), (CONV_HIST, SSD_CONV), (D_MODEL, SSD_N),
                (CONV_HIST, GDN_CONV), (GDN_HV, GDN_DK, GDN_DV))


def _dot(a, b):
    return jnp.dot(a, b, preferred_element_type=F32)


def _dot_nt(a, b):
    return lax.dot_general(a, b, (((1,), (1,)), ((), ())), preferred_element_type=F32)


def _dot_tn(a, b):
    return lax.dot_general(a, b, (((0,), (0,)), ((), ())), preferred_element_type=F32)


def _silu(x):
    return x * jax.nn.sigmoid(x)


def _softplus(x):
    return jnp.maximum(x, 0.0) + jnp.log1p(jnp.exp(-jnp.abs(x)))


def _split3(a):
    hi = a.astype(BF16)
    r = a - hi.astype(F32)
    mid = r.astype(BF16)
    lo = (r - mid.astype(F32)).astype(BF16)
    return hi, mid, lo


def _pack3(a, lane_lt32):
    hi, mid, lo = _split3(a)
    z = jnp.zeros_like(a)
    p = (jnp.where(lane_lt32, hi.astype(F32), z)
         + pltpu.roll(jnp.where(lane_lt32, mid.astype(F32), z), 32, 1)
         + pltpu.roll(jnp.where(lane_lt32, lo.astype(F32), z), 64, 1))
    return p.astype(BF16)


def _causal_conv(buf, s, row, length, w, k_taps):
    acc = buf[s, row:row + length, :] * w[k_taps - 1:k_taps, :]
    for j in range(1, k_taps):
        acc = acc + buf[s, row - j:row - j + length, :] * w[k_taps - 1 - j:k_taps - j, :]
    return acc


def _after(a_bf16, dep):
    u = pltpu.bitcast(dep, jnp.uint32)
    zero = lax.shift_right_logical(lax.shift_right_logical(u, jnp.uint32(16)), jnp.uint32(16))
    bits = pltpu.bitcast(a_bf16, jnp.uint32)
    reps = (bits.shape[0] // zero.shape[0], bits.shape[1] // zero.shape[1])
    return pltpu.bitcast(bits | jnp.tile(zero, reps), BF16)


def _neumann_inverse(xs, eye):
    ps = [eye + x for x in xs]
    xbs = [x.astype(BF16) for x in xs]
    ys = [_dot(xb, xb) for xb in xbs]
    for _ in range(4):
        ybs = [y.astype(BF16) for y in ys]
        rs = [_dot(jnp.concatenate([yb, p.astype(BF16)], axis=0), yb) for yb, p in zip(ybs, ps)]
        ys = [r[:128] for r in rs]
        ps = [p + r[128:] for p, r in zip(ps, rs)]
    return [p + _dot(p.astype(BF16), y.astype(BF16)) for p, y in zip(ps, ys)]


def _layer_kernel(*refs, prompt, n_seq, seq_len, n_sub, final_norm, n_alias):
    tb = n_seq * seq_len
    tbs = tb // n_sub
    n_chunks = tbs // CHUNK
    it = iter(refs)
    x_ref = next(it)
    if not prompt:
        pool_h, sconv_h, ssd_s, gconv_h, gdn_s = (next(it) for _ in range(5))
    (normw, w_a, w_small, w_gdn, w_gate, poolw, poolsc, sconvw, sconvb, sbias, salog, ssd_d, ssdnw, gconvw,
     gdnnw, wbr, wout, fnw, btri, e_dt, e_beta, e_g) = (next(it) for _ in range(22))
    for _ in range(n_alias):
        next(it)
    xo, o_pool, o_sconv, o_ssd, o_gconv, o_gdn = (next(it) for _ in range(6))
    abuf, sbuf, gbuf, yssd, ygdn, st_scr, hn_scr = (next(it) for _ in range(7))

    step = pl.program_id(0)
    if prompt:
        @pl.when(step == 0)
        def _init():
            abuf[:, 0:HEAD_ROWS, :] = jnp.zeros((n_seq, HEAD_ROWS, D_MODEL), F32)
            sbuf[:, 0:HEAD_ROWS, :] = jnp.zeros((n_seq, HEAD_ROWS, SSD_CONV), F32)
            gbuf[:, 0:HEAD_ROWS, :] = jnp.zeros((n_seq, HEAD_ROWS, GDN_CONV), F32)
            st_scr[...] = jnp.zeros(st_scr.shape, F32)
            o_gdn[...] = jnp.zeros(o_gdn.shape, F32)
    else:
        abuf[:, 0:HEAD_ROWS, :] = jnp.zeros((n_seq, HEAD_ROWS, D_MODEL), F32)
        sbuf[:, 0:HEAD_ROWS, :] = jnp.zeros((n_seq, HEAD_ROWS, SSD_CONV), F32)
        gbuf[:, 0:HEAD_ROWS, :] = jnp.zeros((n_seq, HEAD_ROWS, GDN_CONV), F32)
        abuf[:, HEAD_ROWS - POOL_HIST:HEAD_ROWS, :] = pool_h[...]
        sbuf[:, HEAD_ROWS - CONV_HIST:HEAD_ROWS, :] = sconv_h[...]
        gbuf[:, HEAD_ROWS - CONV_HIST:HEAD_ROWS, :] = gconv_h[...]

    row16 = lax.broadcasted_iota(jnp.int32, (HEAD_ROWS, 256), 0).astype(F32)
    lane = lax.broadcasted_iota(jnp.int32, (1, SMALL_W), 1)
    is_decay = (lane < SSD_H) | ((lane >= SSD_H + GDN_HV) & (lane < SSD_H + 2 * GDN_HV))
    lane_lt32 = lane < 32
    q_idx = lax.broadcasted_iota(jnp.int32, (CHUNK, D_MODEL), 0)
    s_idx = lax.broadcasted_iota(jnp.int32, (CHUNK, D_MODEL), 1) & (CHUNK - 1)
    causal = s_idx <= q_idx
    diag_sel = (s_idx == q_idx).astype(F32)
    ri = lax.broadcasted_iota(jnp.int32, (128, 128), 0)
    ci = lax.broadcasted_iota(jnp.int32, (128, 128), 1)
    same = (ri >> 6) == (ci >> 6)
    incl = same & ((ci & 63) <= (ri & 63))
    strict = same & ((ci & 63) < (ri & 63))
    eye = (ri == ci).astype(F32)
    q64 = lax.broadcasted_iota(jnp.int32, (CHUNK, 128), 0)
    l64 = lax.broadcasted_iota(jnp.int32, (CHUNK, 128), 1)
    diag_sel2 = ((l64 & 63) == q64).astype(F32)
    lo_half64 = l64 < CHUNK

    def l2n(t):
        return t * lax.rsqrt(jnp.sum(t * t, axis=-1, keepdims=True) + 1e-6)

    def sub_block(sub):
        r0 = sub * tbs
        rows = slice(r0, r0 + tbs)
        if prompt:
            seqs = [(0, r0, 0, tbs)]
        else:
            seqs = [(s, 0, s * seq_len, seq_len) for s in range(n_seq)]
        chunks_per_seq = seqs[0][3] // CHUNK

        x = x_ref[rows, :]
        hn_scr[rows, :] = (x * lax.rsqrt(jnp.mean(x * x, axis=-1, keepdims=True) + RMS_EPS)
                           * normw[...]).astype(BF16)
        pp = _dot(hn_scr[rows, :], w_a[:, 0:W_POOL])
        sm = _dot(hn_scr[rows, :], w_small[...])

        a_in, z_pool = pp[:, :D_MODEL], pp[:, D_MODEL:]
        y_pool_parts = []
        for s, b0, x0, ln in seqs:
            abuf[s, b0 + HEAD_ROWS:b0 + HEAD_ROWS + ln, :] = a_in[x0:x0 + ln]
            full = abuf[s, b0:b0 + HEAD_ROWS + ln, :]
            o_pool[s] = full[ln + HEAD_ROWS - POOL_HIST:, :]
            ys = []
            for g in range(4):
                win = 2 << g
                fg = full[:, 256 * g:256 * (g + 1)]
                acc = fg
                for k in range(g + 1):
                    acc = acc + pltpu.roll(acc, 1 << k, 0)
                wsum = acc[HEAD_ROWS:]
                cur = fg[HEAD_ROWS:]
                inv_w = 1.0 / win
                if prompt and sub == 0:
                    fac = jnp.where(step == 0, 1.0 / jnp.minimum(row16 + 1.0, float(win)), inv_w)
                    d = jnp.concatenate([wsum[:HEAD_ROWS] * fac - cur[:HEAD_ROWS],
                                         wsum[HEAD_ROWS:] * inv_w - cur[HEAD_ROWS:]], axis=0)
                else:
                    d = wsum * inv_w - cur
                ys.append(_dot(d.astype(BF16), poolw[g]))
            y_pool_parts.append(jnp.concatenate(ys, axis=1))
        y_pool = jnp.concatenate(y_pool_parts, axis=0) if len(seqs) > 1 else y_pool_parts[0]
        y_pool = (y_pool * poolsc[...] * _silu(z_pool)).astype(BF16)

        sp = _softplus(sm + sbias[...])
        sg = jax.nn.sigmoid(sm)
        av = sp * jnp.where(is_decay, -jnp.exp(salog[...]), 0.0)
        a_hi, a_mid, a_lo = _split3(av)
        bt = btri[...]
        acum = _dot(bt, a_hi) + _dot(bt, a_mid) + _dot(bt, a_lo)
        pk_ac = _pack3(acum, lane_lt32)
        dt_exp = _dot(_pack3(sp, lane_lt32), e_dt[...])
        acum_exp = _dot(pk_ac, e_dt[...])
        beta128 = _dot(_pack3(sg, lane_lt32), e_beta[...])
        gc128 = _dot(pk_ac, e_g[...])

        ps = _dot(hn_scr[rows, :], w_a[:, W_POOL:W_POOL + W_SSD])
        u_ssd = jnp.concatenate([ps[:, :D_MODEL], ps[:, 2 * D_MODEL:]], axis=1)
        z_ssd = ps[:, D_MODEL:2 * D_MODEL]
        gate_mid = 3 * D_MODEL // 2
        gate_a = _dot(_after(hn_scr[rows, :], ps[0:8, 0:128]), w_gate[:, 0:gate_mid])
        acts = []
        for s, b0, x0, ln in seqs:
            sbuf[s, b0 + HEAD_ROWS:b0 + HEAD_ROWS + ln, :] = u_ssd[x0:x0 + ln]
            o_sconv[s] = u_ssd[x0 + ln - CONV_HIST:x0 + ln]
            acts.append(_silu(_causal_conv(sbuf, s, b0 + HEAD_ROWS, ln, sconvw[...], 4) + sconvb[...]))
        xbc = jnp.concatenate(acts, axis=0) if len(seqs) > 1 else acts[0]
        xc, bm, cm_ = xbc[:, :D_MODEL], xbc[:, D_MODEL:D_MODEL + 256], xbc[:, D_MODEL + 256:]
        br_pool = _dot(_after(y_pool, xc[0:8, 0:128]), wbr[0])

        st = None
        for c in range(n_chunks):
            r = slice(c * CHUNK, (c + 1) * CHUNK)
            s = c // chunks_per_seq
            if c % chunks_per_seq == 0:
                st = st_scr[...] if prompt else ssd_s[s].T
            ac = acum_exp[r]
            a_row = jnp.sum(ac * diag_sel, axis=0, keepdims=True)
            l_all = jnp.where(causal, jnp.exp(ac - a_row), 0.0)
            bc, cc = bm[r], cm_[r]
            cbt = jnp.concatenate(
                [_dot_nt(cc[:, 128 * g:128 * (g + 1)].astype(BF16),
                         jnp.concatenate([bc[:, 128 * g:128 * (g + 1)].astype(BF16)] * 8, axis=0))
                 for g in range(SSD_G)], axis=1)
            m_all = (cbt * l_all).astype(BF16)
            xdt = xc[r] * dt_exp[r]
            xdtb = xdt.astype(BF16)
            zb = jnp.zeros((CHUNK, 128), BF16)
            y_parts = []
            for j in range(8):
                xp = xdtb[:, 128 * j:128 * (j + 1)]
                rhs = jnp.concatenate([jnp.where(lo_half64, xp, zb), jnp.where(lo_half64, zb, xp)], axis=0)
                y_parts.append(_dot(m_all[:, 128 * j:128 * (j + 1)], rhs))
            y = jnp.concatenate(y_parts, axis=1)
            stb = st.astype(BF16)
            y = y + jnp.exp(ac) * jnp.concatenate(
                [_dot(cc[:, 128 * g:128 * (g + 1)].astype(BF16), stb[:, 512 * g:512 * (g + 1)])
                 for g in range(SSD_G)], axis=1)
            a_last = ac[CHUNK - 1:CHUNK, :]
            xd = (xdt * jnp.exp(a_last - ac)).astype(BF16)
            upd = jnp.concatenate(
                [_dot_tn(bc[:, 128 * g:128 * (g + 1)].astype(BF16), xd[:, 512 * g:512 * (g + 1)])
                 for g in range(SSD_G)], axis=1)
            st = jnp.exp(a_last) * st + upd
            if (c + 1) % chunks_per_seq == 0:
                if prompt:
                    st_scr[...] = st
                else:
                    o_ssd[s] = st.T
            y = y + ssd_d[...] * xc[r]
            yz = y * _silu(z_ssd[r])
            yn = jnp.concatenate(
                [yz[:, 512 * g:512 * (g + 1)]
                 * lax.rsqrt(jnp.mean(yz[:, 512 * g:512 * (g + 1)] ** 2, axis=-1, keepdims=True) + RMS_EPS)
                 for g in range(SSD_G)], axis=1)
            yssd[r0 + c * CHUNK:r0 + (c + 1) * CHUNK, :] = yn * ssdnw[...]
        y_ssd = yssd[rows, :].astype(BF16)

        pg = _dot(hn_scr[rows, :], w_gdn[...])
        u_gdn, z_gdn = pg[:, :GDN_CONV], pg[:, GDN_CONV:]
        gate_b = _dot(_after(hn_scr[rows, :], pg[0:8, 0:128]), w_gate[:, gate_mid:])
        acts = []
        for s, b0, x0, ln in seqs:
            gbuf[s, b0 + HEAD_ROWS:b0 + HEAD_ROWS + ln, :] = u_gdn[x0:x0 + ln]
            o_gconv[s] = u_gdn[x0 + ln - CONV_HIST:x0 + ln]
            acts.append(_silu(_causal_conv(gbuf, s, b0 + HEAD_ROWS, ln, gconvw[...], 4)))
        qkv = jnp.concatenate(acts, axis=0) if len(seqs) > 1 else acts[0]

        qn = [l2n(qkv[:, 128 * j:128 * (j + 1)]) * (GDN_DK ** -0.5) for j in range(4)]
        kn = [l2n(qkv[:, 512 + 128 * j:512 + 128 * (j + 1)]) for j in range(4)]
        vc = qkv[:, 1024:]

        combos = [(c, j) for c in range(n_chunks) for j in range(4)]
        pre = {}
        lps = []
        for c, j in combos:
            r = slice(c * CHUNK, (c + 1) * CHUNK)
            h0, h1 = 2 * j, 2 * j + 1
            kj, qj = kn[j][r], qn[j][r]
            g0 = gc128[r, 128 * h0:128 * (h0 + 1)]
            g1 = gc128[r, 128 * h1:128 * (h1 + 1)]
            kst = jnp.concatenate([kj, kj], axis=0)
            qst = jnp.concatenate([qj, qj], axis=0)
            bst = jnp.concatenate([beta128[r, 128 * h0:128 * (h0 + 1)],
                                   beta128[r, 128 * h1:128 * (h1 + 1)]], axis=0)
            gst = jnp.concatenate([g0, g1], axis=0)
            vst = jnp.concatenate([vc[r, 128 * h0:128 * (h0 + 1)], vc[r, 128 * h1:128 * (h1 + 1)]], axis=0)
            g_row = jnp.sum(jnp.where(lo_half64, g0, g1) * diag_sel2, axis=0, keepdims=True)
            dec = jnp.exp(gst - g_row)
            eg = jnp.exp(gst)
            kq = _dot_nt(jnp.concatenate([kj, qj], axis=0).astype(BF16), kst.astype(BF16))
            kk2 = jnp.concatenate([kq[:CHUNK], kq[:CHUNK]], axis=0)
            qk2 = jnp.concatenate([kq[CHUNK:], kq[CHUNK:]], axis=0)
            lps.append(jnp.where(strict, kk2 * bst * dec, 0.0))
            gl0, gl1 = g0[CHUNK - 1:CHUNK, :], g1[CHUNK - 1:CHUNK, :]
            pre[(c, j)] = dict(
                ap=jnp.where(incl, qk2 * dec, 0.0).astype(BF16),
                rhs=jnp.concatenate([vst * bst, kst * bst * eg], axis=1).astype(BF16),
                qg=(qst * eg).astype(BF16),
                kjb=kj.astype(BF16),
                e_st=jnp.exp(jnp.concatenate([gl0 - g0, gl1 - g1], axis=0)),
                dl0=jnp.exp(gl0), dl1=jnp.exp(gl1))
        t_invs = _neumann_inverse([-lp for lp in lps], eye)
        for (c, j), t_inv in zip(combos, t_invs):
            uw = _dot(t_inv.astype(BF16), pre[(c, j)]["rhs"])
            pre[(c, j)]["u"] = uw[:, :128]
            pre[(c, j)]["w"] = uw[:, 128:].astype(BF16)

        gs = [None] * GDN_HV
        for c in range(n_chunks):
            s = c // chunks_per_seq
            if c % chunks_per_seq == 0:
                for h in range(GDN_HV):
                    gs[h] = o_gdn[0, h] if prompt else gdn_s[s, h]
            rr = []
            for j in range(4):
                p_ = pre[(c, j)]
                for hh in range(2):
                    half = slice(hh * CHUNK, (hh + 1) * CHUNK)
                    rr.append(_dot(jnp.concatenate([p_["w"][half], p_["qg"][half]], axis=0),
                                   gs[2 * j + hh].astype(BF16)))
            vnews = [pre[(c, j)]["u"] - jnp.concatenate([rr[2 * j][:CHUNK], rr[2 * j + 1][:CHUNK]], axis=0)
                     for j in range(4)]
            upds = []
            for j in range(4):
                vn_e = vnews[j] * pre[(c, j)]["e_st"]
                upds.append(_dot_tn(pre[(c, j)]["kjb"],
                                    jnp.concatenate([vn_e[:CHUNK], vn_e[CHUNK:]], axis=1).astype(BF16)))
            for j in range(4):
                h0, h1 = 2 * j, 2 * j + 1
                gs[h0] = pre[(c, j)]["dl0"] * gs[h0] + upds[j][:, :128]
                gs[h1] = pre[(c, j)]["dl1"] * gs[h1] + upds[j][:, 128:]
            yr = slice(r0 + c * CHUNK, r0 + (c + 1) * CHUNK)
            for j in range(4):
                h0, h1 = 2 * j, 2 * j + 1
                o_st = (jnp.concatenate([rr[2 * j][CHUNK:], rr[2 * j + 1][CHUNK:]], axis=0)
                        + _dot(pre[(c, j)]["ap"], vnews[j].astype(BF16)))
                ygdn[yr, 128 * h0:128 * (h0 + 1)] = o_st[:CHUNK]
                ygdn[yr, 128 * h1:128 * (h1 + 1)] = o_st[CHUNK:]
            if (c + 1) % chunks_per_seq == 0:
                for h in range(GDN_HV):
                    o_gdn[s, h] = gs[h]
        og = ygdn[rows, :]
        og = jnp.concatenate(
            [og[:, 128 * h:128 * (h + 1)]
             * lax.rsqrt(jnp.mean(og[:, 128 * h:128 * (h + 1)] ** 2, axis=-1, keepdims=True) + RMS_EPS)
             for h in range(GDN_HV)], axis=1)
        y_gdn = (og * gdnnw[...] * _silu(z_gdn)).astype(BF16)

        gates = jax.nn.sigmoid(jnp.concatenate([gate_a, gate_b], axis=1))
        merged = (gates[:, :D_MODEL] * br_pool
                  + gates[:, D_MODEL:2 * D_MODEL] * _dot(y_ssd, wbr[1])
                  + gates[:, 2 * D_MODEL:] * _dot(y_gdn, wbr[2]))
        xn = x + _dot(merged.astype(BF16), wout[...])
        if final_norm:
            xn = xn * lax.rsqrt(jnp.mean(xn * xn, axis=-1, keepdims=True) + RMS_EPS) * fnw[...]
        xo[rows, :] = xn

    for sub in range(n_sub):
        sub_block(sub)

    if prompt:
        abuf[0, 0:HEAD_ROWS, :] = abuf[0, tb:tb + HEAD_ROWS, :]
        sbuf[0, 0:HEAD_ROWS, :] = sbuf[0, tb:tb + HEAD_ROWS, :]
        gbuf[0, 0:HEAD_ROWS, :] = gbuf[0, tb:tb + HEAD_ROWS, :]

        @pl.when(step == pl.num_programs(0) - 1)
        def _final_state():
            o_ssd[0] = st_scr[...].T


def _layer_spec(arr, layer):
    nd = arr.ndim - 1
    return pl.BlockSpec((None,) + arr.shape[1:], lambda i, _l=layer, _nd=nd: (_l,) + (0,) * _nd,
                        pipeline_mode=pl.Buffered(1))


def _const_spec(arr):
    nd = arr.ndim
    return pl.BlockSpec(arr.shape, lambda i, _nd=nd: (0,) * _nd, pipeline_mode=pl.Buffered(1))


def _run_layer(layer, x2d, states, weights, fnw, consts, stacked, *, prompt, final_norm):
    tokens = x2d.shape[0]
    if prompt:
        n_seq, seq_len, n_state, n_sub = 1, PROMPT_BLOCK, 1, PROMPT_SUB_BLOCKS
    else:
        n_seq, seq_len, n_state, n_sub = SAMPLE_SEQS, CHUNK, tokens // CHUNK, 1
    tb = n_seq * seq_len
    grid = (tokens // tb,)

    def state_spec(shape):
        nd = len(shape)
        if prompt:
            return pl.BlockSpec((None, 1) + shape, lambda i, _nd=nd: (layer, 0) + (0,) * _nd)
        return pl.BlockSpec((None, n_seq) + shape, lambda i, _nd=nd: (layer, i) + (0,) * _nd)

    in_specs = [pl.BlockSpec((tb, D_MODEL), lambda i: (i, 0))]
    args = [x2d]
    if not prompt:
        in_specs += [state_spec(s) for s in STATE_SHAPES]
        args += list(states)
    in_specs += [_layer_spec(w, layer) for w in weights] + [_const_spec(fnw)] + [_const_spec(c) for c in consts]
    args += list(weights) + [fnw] + list(consts)
    aliases = {}
    for k, buf in enumerate(stacked):
        aliases[len(args)] = 1 + k
        in_specs.append(pl.BlockSpec(memory_space=pl.ANY))
        args.append(buf)

    out_shape = [jax.ShapeDtypeStruct((tokens, D_MODEL), F32)]
    out_shape += [jax.ShapeDtypeStruct((DEPTH, n_state) + s, F32) for s in STATE_SHAPES]
    out_specs = [pl.BlockSpec((tb, D_MODEL), lambda i: (i, 0))] + [state_spec(s) for s in STATE_SHAPES]

    scratch = [pltpu.VMEM((n_seq, HEAD_ROWS + seq_len, D_MODEL), F32),
               pltpu.VMEM((n_seq, HEAD_ROWS + seq_len, SSD_CONV), F32),
               pltpu.VMEM((n_seq, HEAD_ROWS + seq_len, GDN_CONV), F32),
               pltpu.VMEM((tb, D_MODEL), F32),
               pltpu.VMEM((tb, D_MODEL), F32),
               pltpu.VMEM((SSD_N, D_MODEL), F32),
               pltpu.VMEM((tb, D_MODEL), BF16)]
    kern = functools.partial(_layer_kernel, prompt=prompt, n_seq=n_seq, seq_len=seq_len, n_sub=n_sub,
                             final_norm=final_norm, n_alias=len(aliases))
    return pl.pallas_call(
        kern, grid=grid, in_specs=in_specs, out_specs=out_specs, out_shape=out_shape,
        scratch_shapes=scratch, input_output_aliases=aliases,
        compiler_params=pltpu.CompilerParams(dimension_semantics=("arbitrary",),
                                             vmem_limit_bytes=VMEM_LIMIT_BYTES),
    )(*args)


def _expand_matrix(first_lane, n_heads, width):
    rows = jnp.arange(SMALL_W)
    src = rows % 32
    head = src - first_lane
    valid = (rows < 96) & (head >= 0) & (head < n_heads)
    cols = jnp.arange(n_heads * width) // width
    return (valid[:, None] & (head[:, None] == cols[None, :])).astype(BF16)


def _block_tri(tb):
    i = jnp.arange(tb)
    return ((i[:, None] // CHUNK == i[None, :] // CHUNK) & (i[None, :] <= i[:, None])).astype(BF16)


def _all_layer_weights(norm_w, w_in, pool_w, pool_scale, ssd_conv_w, ssd_conv_b, ssd_dt_bias, ssd_A_log,
                       ssd_D, ssd_norm_w, gdn_conv_w, gdn_dt_bias, gdn_A_log, gdn_norm_w, w_br_pool,
                       w_br_ssd, w_br_gdn, w_out):
    n_pad = SMALL_W - SSD_H - 2 * GDN_HV
    w_a = w_in[:, :, 0:IN_DT].astype(BF16)
    w_gdn = w_in[:, :, IN_GDN:IN_BETA].astype(BF16)
    w_gate = w_in[:, :, IN_GATE:IN_END].astype(BF16)
    w_small = jnp.concatenate([w_in[:, :, IN_DT:IN_GDN], w_in[:, :, IN_BETA:IN_GATE],
                               jnp.zeros((DEPTH, D_MODEL, n_pad), F32)], axis=2).astype(BF16)
    pad8 = jnp.zeros((DEPTH, GDN_HV), F32)
    pad = jnp.zeros((DEPTH, n_pad), F32)
    sbias = jnp.concatenate([ssd_dt_bias, pad8, gdn_dt_bias, pad], axis=1)[:, None, :]
    salog = jnp.concatenate([ssd_A_log, pad8, gdn_A_log, pad], axis=1)[:, None, :]
    return [
        norm_w[:, None, :], w_a, w_small, w_gdn, w_gate, pool_w.astype(BF16), pool_scale[:, None, :],
        ssd_conv_w, ssd_conv_b[:, None, :], sbias, salog,
        jnp.repeat(ssd_D, SSD_P, axis=1)[:, None, :], ssd_norm_w[:, None, :],
        gdn_conv_w, jnp.tile(gdn_norm_w, (1, GDN_HV))[:, None, :],
        jnp.stack([w_br_pool, w_br_ssd, w_br_gdn], axis=1).astype(BF16), w_out.astype(BF16),
    ]


def kernel(x_prompt, x_sample, state_pool, state_ssd_conv, state_ssd, state_gdn_conv, state_gdn, norm_w, w_in,
           pool_w, pool_scale, ssd_conv_w, ssd_conv_b, ssd_dt_bias, ssd_A_log, ssd_D, ssd_norm_w, gdn_conv_w,
           gdn_dt_bias, gdn_A_log, gdn_norm_w, w_br_pool, w_br_ssd, w_br_gdn, w_out, final_norm_w):
    bp, seq, _ = x_prompt.shape
    bs, dseq, _ = x_sample.shape
    assert bp == 1 and dseq == CHUNK and seq % PROMPT_BLOCK == 0 and bs % SAMPLE_SEQS == 0
    e_mats = [_expand_matrix(0, SSD_H, SSD_P), _expand_matrix(SSD_H, GDN_HV, GDN_DK),
              _expand_matrix(SSD_H + GDN_HV, GDN_HV, GDN_DK)]
    consts_p = [_block_tri(PROMPT_BLOCK // PROMPT_SUB_BLOCKS)] + e_mats
    consts_s = [_block_tri(SAMPLE_SEQS * CHUNK)] + e_mats
    weights = _all_layer_weights(norm_w, w_in, pool_w, pool_scale, ssd_conv_w, ssd_conv_b, ssd_dt_bias,
                                 ssd_A_log, ssd_D, ssd_norm_w, gdn_conv_w, gdn_dt_bias, gdn_A_log, gdn_norm_w,
                                 w_br_pool, w_br_ssd, w_br_gdn, w_out)
    fnw = final_norm_w[None, :]
    states_s = (state_pool, state_ssd_conv, state_ssd.reshape(DEPTH, bs, D_MODEL, SSD_N),
                state_gdn_conv, state_gdn)

    hp = x_prompt.reshape(seq, D_MODEL)
    hs = x_sample.reshape(bs * dseq, D_MODEL)
    stk_p = [jnp.zeros((DEPTH, 1) + s, F32) for s in STATE_SHAPES]
    stk_s = [jnp.zeros((DEPTH, bs) + s, F32) for s in STATE_SHAPES]
    for l in range(DEPTH):
        last = l == DEPTH - 1
        rp = _run_layer(l, hp, None, weights, fnw, consts_p, stk_p, prompt=True, final_norm=last)
        rs = _run_layer(l, hs, states_s, weights, fnw, consts_s, stk_s, prompt=False, final_norm=last)
        hp, stk_p = rp[0], rp[1:]
        hs, stk_s = rs[0], rs[1:]

    def ssd_out(st):
        return st.reshape(DEPTH, st.shape[1], SSD_H, SSD_P, SSD_N)

    return (hp.reshape(bp, seq, D_MODEL), hs.reshape(bs, dseq, D_MODEL),
            stk_p[0], stk_s[0], stk_p[1], stk_s[1], ssd_out(stk_p[2]), ssd_out(stk_s[2]),
            stk_p[3], stk_s[3], stk_p[4], stk_s[4])
```

```python
import functools

import jax
import jax.numpy as jnp
from jax import lax
from jax.experimental import pallas as pl
from jax.experimental.pallas import tpu as pltpu

F32 = jnp.float32
BF16 = jnp.bfloat16

D_MODEL = 1024
DEPTH = 4
CHUNK = 64
RMS_EPS = 1e-6
POOL_HIST = 15
CONV_HIST = 3
HEAD_ROWS = 16
SSD_H, SSD_P, SSD_N, SSD_G = 16, 64, 128, 2
SSD_CONV = 1536
GDN_HV, GDN_DK, GDN_DV = 8, 128, 128
GDN_CONV = 2048
SMALL_W = 128

W_POOL, W_SSD = 2048, 2560
IN_DT, IN_GDN, IN_BETA, IN_GATE, IN_END = 4608, 4624, 7696, 7712, 10784

PROMPT_BLOCK = 256
PROMPT_SUB_BLOCKS = 1
SAMPLE_SEQS = 2
VMEM_LIMIT_BYTES = 60 * 1024 * 1024

STATE_SHAPES = ((POOL_HIST, D_MODEL), (CONV_HIST, SSD_CONV), (D_MODEL, SSD_N),
                (CONV_HIST, GDN_CONV), (GDN_HV, GDN_DK, GDN_DV))


def _dot(a, b):
    return jnp.dot(a, b, preferred_element_type=F32)


def _dot_nt(a, b):
    return lax.dot_general(a, b, (((1,), (1,)), ((), ())), preferred_element_type=F32)


def _dot_tn(a, b):
    return lax.dot_general(a, b, (((0,), (0,)), ((), ())), preferred_element_type=F32)


def _silu(x):
    return x * jax.nn.sigmoid(x)


def _softplus(x):
    return jnp.maximum(x, 0.0) + jnp.log1p(jnp.exp(-jnp.abs(x)))


def _split3(a):
    hi = a.astype(BF16)
    r = a - hi.astype(F32)
    mid = r.astype(BF16)
    lo = (r - mid.astype(F32)).astype(BF16)
    return hi, mid, lo


def _pack3(a, lane_lt32):
    hi, mid, lo = _split3(a)
    z = jnp.zeros_like(a)
    p = (jnp.where(lane_lt32, hi.astype(F32), z)
         + pltpu.roll(jnp.where(lane_lt32, mid.astype(F32), z), 32, 1)
         + pltpu.roll(jnp.where(lane_lt32, lo.astype(F32), z), 64, 1))
    return p.astype(BF16)


def _causal_conv(buf, s, row, length, w, k_taps):
    acc = buf[s, row:row + length, :] * w[k_taps - 1:k_taps, :]
    for j in range(1, k_taps):
        acc = acc + buf[s, row - j:row - j + length, :] * w[k_taps - 1 - j:k_taps - j, :]
    return acc


def _after(a_bf16, dep):
    u = pltpu.bitcast(dep, jnp.uint32)
    zero = lax.shift_right_logical(lax.shift_right_logical(u, jnp.uint32(16)), jnp.uint32(16))
    bits = pltpu.bitcast(a_bf16, jnp.uint32)
    reps = (bits.shape[0] // zero.shape[0], bits.shape[1] // zero.shape[1])
    return pltpu.bitcast(bits | jnp.tile(zero, reps), BF16)


def _neumann_inverse(xs, eye):
    ps = [eye + x for x in xs]
    xbs = [x.astype(BF16) for x in xs]
    ys = [_dot(xb, xb) for xb in xbs]
    for _ in range(4):
        ybs = [y.astype(BF16) for y in ys]
        rs = [_dot(jnp.concatenate([yb, p.astype(BF16)], axis=0), yb) for yb, p in zip(ybs, ps)]
        ys = [r[:128] for r in rs]
        ps = [p + r[128:] for p, r in zip(ps, rs)]
    return [p + _dot(p.astype(BF16), y.astype(BF16)) for p, y in zip(ps, ys)]


def _layer_kernel(*refs, prompt, n_seq, seq_len, n_sub, final_norm, n_alias):
    tb = n_seq * seq_len
    tbs = tb // n_sub
    n_chunks = tbs // CHUNK
    it = iter(refs)
    x_ref = next(it)
    if not prompt:
        pool_h, sconv_h, ssd_s, gconv_h, gdn_s = (next(it) for _ in range(5))
    (normw, w_a, w_small, w_gdn, w_gate, poolw, poolsc, sconvw, sconvb, sbias, salog, ssd_d, ssdnw, gconvw,
     gdnnw, wbr, wout, fnw, btri, e_dt, e_beta, e_g) = (next(it) for _ in range(22))
    for _ in range(n_alias):
        next(it)
    xo, o_pool, o_sconv, o_ssd, o_gconv, o_gdn = (next(it) for _ in range(6))
    abuf, sbuf, gbuf, yssd, ygdn, st_scr, hn_scr = (next(it) for _ in range(7))

    step = pl.program_id(0)
    if prompt:
        @pl.when(step == 0)
        def _init():
            abuf[:, 0:HEAD_ROWS, :] = jnp.zeros((n_seq, HEAD_ROWS, D_MODEL), F32)
            sbuf[:, 0:HEAD_ROWS, :] = jnp.zeros((n_seq, HEAD_ROWS, SSD_CONV), F32)
            gbuf[:, 0:HEAD_ROWS, :] = jnp.zeros((n_seq, HEAD_ROWS, GDN_CONV), F32)
            st_scr[...] = jnp.zeros(st_scr.shape, F32)
            o_gdn[...] = jnp.zeros(o_gdn.shape, F32)
    else:
        abuf[:, 0:HEAD_ROWS, :] = jnp.zeros((n_seq, HEAD_ROWS, D_MODEL), F32)
        sbuf[:, 0:HEAD_ROWS, :] = jnp.zeros((n_seq, HEAD_ROWS, SSD_CONV), F32)
        gbuf[:, 0:HEAD_ROWS, :] = jnp.zeros((n_seq, HEAD_ROWS, GDN_CONV), F32)
        abuf[:, HEAD_ROWS - POOL_HIST:HEAD_ROWS, :] = pool_h[...]
        sbuf[:, HEAD_ROWS - CONV_HIST:HEAD_ROWS, :] = sconv_h[...]
        gbuf[:, HEAD_ROWS - CONV_HIST:HEAD_ROWS, :] = gconv_h[...]

    row16 = lax.broadcasted_iota(jnp.int32, (HEAD_ROWS, 256), 0).astype(F32)
    lane = lax.broadcasted_iota(jnp.int32, (1, SMALL_W), 1)
    is_decay = (lane < SSD_H) | ((lane >= SSD_H + GDN_HV) & (lane < SSD_H + 2 * GDN_HV))
    lane_lt32 = lane < 32
    q_idx = lax.broadcasted_iota(jnp.int32, (CHUNK, D_MODEL), 0)
    s_idx = lax.broadcasted_iota(jnp.int32, (CHUNK, D_MODEL), 1) & (CHUNK - 1)
    causal = s_idx <= q_idx
    diag_sel = (s_idx == q_idx).astype(F32)
    ri = lax.broadcasted_iota(jnp.int32, (128, 128), 0)
    ci = lax.broadcasted_iota(jnp.int32, (128, 128), 1)
    same = (ri >> 6) == (ci >> 6)
    incl = same & ((ci & 63) <= (ri & 63))
    strict = same & ((ci & 63) < (ri & 63))
    eye = (ri == ci).astype(F32)
    q64 = lax.broadcasted_iota(jnp.int32, (CHUNK, 128), 0)
    l64 = lax.broadcasted_iota(jnp.int32, (CHUNK, 128), 1)
    diag_sel2 = ((l64 & 63) == q64).astype(F32)
    lo_half64 = l64 < CHUNK

    def l2n(t):
        return t * lax.rsqrt(jnp.sum(t * t, axis=-1, keepdims=True) + 1e-6)

    def sub_block(sub):
        r0 = sub * tbs
        rows = slice(r0, r0 + tbs)
        if prompt:
            seqs = [(0, r0, 0, tbs)]
        else:
            seqs = [(s, 0, s * seq_len, seq_len) for s in range(n_seq)]
        chunks_per_seq = seqs[0][3] // CHUNK

        x = x_ref[rows, :]
        hn_scr[rows, :] = (x * lax.rsqrt(jnp.mean(x * x, axis=-1, keepdims=True) + RMS_EPS)
                           * normw[...]).astype(BF16)
        pp = _dot(hn_scr[rows, :], w_a[:, 0:W_POOL])
        sm = _dot(hn_scr[rows, :], w_small[...])

        a_in, z_pool = pp[:, :D_MODEL], pp[:, D_MODEL:]
        y_pool_parts = []
        for s, b0, x0, ln in seqs:
            abuf[s, b0 + HEAD_ROWS:b0 + HEAD_ROWS + ln, :] = a_in[x0:x0 + ln]
            full = abuf[s, b0:b0 + HEAD_ROWS + ln, :]
            o_pool[s] = full[ln + HEAD_ROWS - POOL_HIST:, :]
            ys = []
            for g in range(4):
                win = 2 << g
                fg = full[:, 256 * g:256 * (g + 1)]
                acc = fg
                for k in range(g + 1):
                    acc = acc + pltpu.roll(acc, 1 << k, 0)
                wsum = acc[HEAD_ROWS:]
                cur = fg[HEAD_ROWS:]
                inv_w = 1.0 / win
                if prompt and sub == 0:
                    fac = jnp.where(step == 0, 1.0 / jnp.minimum(row16 + 1.0, float(win)), inv_w)
                    d = jnp.concatenate([wsum[:HEAD_ROWS] * fac - cur[:HEAD_ROWS],
                                         wsum[HEAD_ROWS:] * inv_w - cur[HEAD_ROWS:]], axis=0)
                else:
                    d = wsum * inv_w - cur
                ys.append(_dot(d.astype(BF16), poolw[g]))
            y_pool_parts.append(jnp.concatenate(ys, axis=1))
        y_pool = jnp.concatenate(y_pool_parts, axis=0) if len(seqs) > 1 else y_pool_parts[0]
        y_pool = (y_pool * poolsc[...] * _silu(z_pool)).astype(BF16)

        sp = _softplus(sm + sbias[...])
        sg = jax.nn.sigmoid(sm)
        av = sp * jnp.where(is_decay, -jnp.exp(salog[...]), 0.0)
        a_hi, a_mid, a_lo = _split3(av)
        bt = btri[...]
        acum = _dot(bt, a_hi) + _dot(bt, a_mid) + _dot(bt, a_lo)
        pk_ac = _pack3(acum, lane_lt32)
        dt_exp = _dot(_pack3(sp, lane_lt32), e_dt[...])
        acum_exp = _dot(pk_ac, e_dt[...])
        beta128 = _dot(_pack3(sg, lane_lt32), e_beta[...])
        gc128 = _dot(pk_ac, e_g[...])

        ps = _dot(hn_scr[rows, :], w_a[:, W_POOL:W_POOL + W_SSD])
        u_ssd = jnp.concatenate([ps[:, :D_MODEL], ps[:, 2 * D_MODEL:]], axis=1)
        z_ssd = ps[:, D_MODEL:2 * D_MODEL]
        gate_mid = 3 * D_MODEL // 2
        gate_a = _dot(_after(hn_scr[rows, :], ps[0:8, 0:128]), w_gate[:, 0:gate_mid])
        acts = []
        for s, b0, x0, ln in seqs:
            sbuf[s, b0 + HEAD_ROWS:b0 + HEAD_ROWS + ln, :] = u_ssd[x0:x0 + ln]
            o_sconv[s] = u_ssd[x0 + ln - CONV_HIST:x0 + ln]
            acts.append(_silu(_causal_conv(sbuf, s, b0 + HEAD_ROWS, ln, sconvw[...], 4) + sconvb[...]))
        xbc = jnp.concatenate(acts, axis=0) if len(seqs) > 1 else acts[0]
        xc, bm, cm_ = xbc[:, :D_MODEL], xbc[:, D_MODEL:D_MODEL + 256], xbc[:, D_MODEL + 256:]
        br_pool = _dot(_after(y_pool, xc[0:8, 0:128]), wbr[0])

        cr = [slice(c * CHUNK, (c + 1) * CHUNK) for c in range(n_chunks)]
        acs = [acum_exp[r] for r in cr]
        a_rows = [jnp.sum(ac * diag_sel, axis=0, keepdims=True) for ac in acs]
        l_alls = [jnp.where(causal, jnp.exp(ac - a_row), 0.0) for ac, a_row in zip(acs, a_rows)]
        bcs = [bm[r].astype(BF16) for r in cr]
        ccs = [cm_[r].astype(BF16) for r in cr]
        cbts = [jnp.concatenate(
            [_dot_nt(cc[:, 128 * g:128 * (g + 1)], jnp.concatenate([bc[:, 128 * g:128 * (g + 1)]] * 8, axis=0))
             for g in range(SSD_G)], axis=1) for bc, cc in zip(bcs, ccs)]
        m_alls = [(cbt * l_all).astype(BF16) for cbt, l_all in zip(cbts, l_alls)]
        xdts = [xc[r] * dt_exp[r] for r in cr]
        zb = jnp.zeros((CHUNK, 128), BF16)
        y_intras = []
        for m_all, xdt in zip(m_alls, xdts):
            xdtb = xdt.astype(BF16)
            y_parts = []
            for j in range(8):
                xp = xdtb[:, 128 * j:128 * (j + 1)]
                rhs = jnp.concatenate([jnp.where(lo_half64, xp, zb), jnp.where(lo_half64, zb, xp)], axis=0)
                y_parts.append(_dot(m_all[:, 128 * j:128 * (j + 1)], rhs))
            y_intras.append(jnp.concatenate(y_parts, axis=1))
        a_lasts = [ac[CHUNK - 1:CHUNK, :] for ac in acs]
        xds = [(xdt * jnp.exp(a_last - ac)).astype(BF16) for xdt, a_last, ac in zip(xdts, a_lasts, acs)]
        upds = [jnp.concatenate(
            [_dot_tn(bc[:, 128 * g:128 * (g + 1)], xd[:, 512 * g:512 * (g + 1)]) for g in range(SSD_G)], axis=1)
            for bc, xd in zip(bcs, xds)]

        st = None
        for c in range(n_chunks):
            r = cr[c]
            s = c // chunks_per_seq
            if c % chunks_per_seq == 0:
                st = st_scr[...] if prompt else ssd_s[s].T
            stb = st.astype(BF16)
            y = y_intras[c] + jnp.exp(acs[c]) * jnp.concatenate(
                [_dot(ccs[c][:, 128 * g:128 * (g + 1)], stb[:, 512 * g:512 * (g + 1)])
                 for g in range(SSD_G)], axis=1)
            st = jnp.exp(a_lasts[c]) * st + upds[c]
            if (c + 1) % chunks_per_seq == 0:
                if prompt:
                    st_scr[...] = st
                else:
                    o_ssd[s] = st.T
            y = y + ssd_d[...] * xc[r]
            yz = y * _silu(z_ssd[r])
            yn = jnp.concatenate(
                [yz[:, 512 * g:512 * (g + 1)]
                 * lax.rsqrt(jnp.mean(yz[:, 512 * g:512 * (g + 1)] ** 2, axis=-1, keepdims=True) + RMS_EPS)
                 for g in range(SSD_G)], axis=1)
            yssd[r0 + c * CHUNK:r0 + (c + 1) * CHUNK, :] = yn * ssdnw[...]
        y_ssd = yssd[rows, :].astype(BF16)

        pg = _dot(hn_scr[rows, :], w_gdn[...])
        u_gdn, z_gdn = pg[:, :GDN_CONV], pg[:, GDN_CONV:]
        gate_b = _dot(_after(hn_scr[rows, :], pg[0:8, 0:128]), w_gate[:, gate_mid:])
        acts = []
        for s, b0, x0, ln in seqs:
            gbuf[s, b0 + HEAD_ROWS:b0 + HEAD_ROWS + ln, :] = u_gdn[x0:x0 + ln]
            o_gconv[s] = u_gdn[x0 + ln - CONV_HIST:x0 + ln]
            acts.append(_silu(_causal_conv(gbuf, s, b0 + HEAD_ROWS, ln, gconvw[...], 4)))
        qkv = jnp.concatenate(acts, axis=0) if len(seqs) > 1 else acts[0]

        qn = [l2n(qkv[:, 128 * j:128 * (j + 1)]) * (GDN_DK ** -0.5) for j in range(4)]
        kn = [l2n(qkv[:, 512 + 128 * j:512 + 128 * (j + 1)]) for j in range(4)]
        vc = qkv[:, 1024:]

        combos = [(c, j) for c in range(n_chunks) for j in range(4)]
        pre = {}
        lps = []
        for c, j in combos:
            r = slice(c * CHUNK, (c + 1) * CHUNK)
            h0, h1 = 2 * j, 2 * j + 1
            kj, qj = kn[j][r], qn[j][r]
            g0 = gc128[r, 128 * h0:128 * (h0 + 1)]
            g1 = gc128[r, 128 * h1:128 * (h1 + 1)]
            kst = jnp.concatenate([kj, kj], axis=0)
            qst = jnp.concatenate([qj, qj], axis=0)
            bst = jnp.concatenate([beta128[r, 128 * h0:128 * (h0 + 1)],
                                   beta128[r, 128 * h1:128 * (h1 + 1)]], axis=0)
            gst = jnp.concatenate([g0, g1], axis=0)
            vst = jnp.concatenate([vc[r, 128 * h0:128 * (h0 + 1)], vc[r, 128 * h1:128 * (h1 + 1)]], axis=0)
            g_row = jnp.sum(jnp.where(lo_half64, g0, g1) * diag_sel2, axis=0, keepdims=True)
            dec = jnp.exp(gst - g_row)
            eg = jnp.exp(gst)
            kq = _dot_nt(jnp.concatenate([kj, qj], axis=0).astype(BF16), kst.astype(BF16))
            kk2 = jnp.concatenate([kq[:CHUNK], kq[:CHUNK]], axis=0)
            qk2 = jnp.concatenate([kq[CHUNK:], kq[CHUNK:]], axis=0)
            lps.append(jnp.where(strict, kk2 * bst * dec, 0.0))
            gl0, gl1 = g0[CHUNK - 1:CHUNK, :], g1[CHUNK - 1:CHUNK, :]
            pre[(c, j)] = dict(
                ap=jnp.where(incl, qk2 * dec, 0.0).astype(BF16),
                rhs=jnp.concatenate([vst * bst, kst * bst * eg], axis=1).astype(BF16),
                qg=(qst * eg).astype(BF16),
                kjb=kj.astype(BF16),
                e_st=jnp.exp(jnp.concatenate([gl0 - g0, gl1 - g1], axis=0)),
                dl0=jnp.exp(gl0), dl1=jnp.exp(gl1))
        t_invs = _neumann_inverse([-lp for lp in lps], eye)
        for (c, j), t_inv in zip(combos, t_invs):
            uw = _dot(t_inv.astype(BF16), pre[(c, j)]["rhs"])
            pre[(c, j)]["u"] = uw[:, :128]
            pre[(c, j)]["w"] = uw[:, 128:].astype(BF16)

        gs = [None] * GDN_HV
        for c in range(n_chunks):
            s = c // chunks_per_seq
            if c % chunks_per_seq == 0:
                for h in range(GDN_HV):
                    gs[h] = o_gdn[0, h] if prompt else gdn_s[s, h]
            rr = []
            for j in range(4):
                p_ = pre[(c, j)]
                for hh in range(2):
                    half = slice(hh * CHUNK, (hh + 1) * CHUNK)
                    rr.append(_dot(jnp.concatenate([p_["w"][half], p_["qg"][half]], axis=0),
                                   gs[2 * j + hh].astype(BF16)))
            vnews = [pre[(c, j)]["u"] - jnp.concatenate([rr[2 * j][:CHUNK], rr[2 * j + 1][:CHUNK]], axis=0)
                     for j in range(4)]
            upds = []
            for j in range(4):
                vn_e = vnews[j] * pre[(c, j)]["e_st"]
                upds.append(_dot_tn(pre[(c, j)]["kjb"],
                                    jnp.concatenate([vn_e[:CHUNK], vn_e[CHUNK:]], axis=1).astype(BF16)))
            for j in range(4):
                h0, h1 = 2 * j, 2 * j + 1
                gs[h0] = pre[(c, j)]["dl0"] * gs[h0] + upds[j][:, :128]
                gs[h1] = pre[(c, j)]["dl1"] * gs[h1] + upds[j][:, 128:]
            yr = slice(r0 + c * CHUNK, r0 + (c + 1) * CHUNK)
            for j in range(4):
                h0, h1 = 2 * j, 2 * j + 1
                o_st = (jnp.concatenate([rr[2 * j][CHUNK:], rr[2 * j + 1][CHUNK:]], axis=0)
                        + _dot(pre[(c, j)]["ap"], vnews[j].astype(BF16)))
                ygdn[yr, 128 * h0:128 * (h0 + 1)] = o_st[:CHUNK]
                ygdn[yr, 128 * h1:128 * (h1 + 1)] = o_st[CHUNK:]
            if (c + 1) % chunks_per_seq == 0:
                for h in range(GDN_HV):
                    o_gdn[s, h] = gs[h]
        og = ygdn[rows, :]
        og = jnp.concatenate(
            [og[:, 128 * h:128 * (h + 1)]
             * lax.rsqrt(jnp.mean(og[:, 128 * h:128 * (h + 1)] ** 2, axis=-1, keepdims=True) + RMS_EPS)
             for h in range(GDN_HV)], axis=1)
        y_gdn = (og * gdnnw[...] * _silu(z_gdn)).astype(BF16)

        gates = jax.nn.sigmoid(jnp.concatenate([gate_a, gate_b], axis=1))
        merged = (gates[:, :D_MODEL] * br_pool
                  + gates[:, D_MODEL:2 * D_MODEL] * _dot(y_ssd, wbr[1])
                  + gates[:, 2 * D_MODEL:] * _dot(y_gdn, wbr[2]))
        xn = x + _dot(merged.astype(BF16), wout[...])
        if final_norm:
            xn = xn * lax.rsqrt(jnp.mean(xn * xn, axis=-1, keepdims=True) + RMS_EPS) * fnw[...]
        xo[rows, :] = xn

    for sub in range(n_sub):
        sub_block(sub)

    if prompt:
        abuf[0, 0:HEAD_ROWS, :] = abuf[0, tb:tb + HEAD_ROWS, :]
        sbuf[0, 0:HEAD_ROWS, :] = sbuf[0, tb:tb + HEAD_ROWS, :]
        gbuf[0, 0:HEAD_ROWS, :] = gbuf[0, tb:tb + HEAD_ROWS, :]

        @pl.when(step == pl.num_programs(0) - 1)
        def _final_state():
            o_ssd[0] = st_scr[...].T


def _layer_spec(arr, layer):
    nd = arr.ndim - 1
    return pl.BlockSpec((None,) + arr.shape[1:], lambda i, _l=layer, _nd=nd: (_l,) + (0,) * _nd,
                        pipeline_mode=pl.Buffered(1))


def _const_spec(arr):
    nd = arr.ndim
    return pl.BlockSpec(arr.shape, lambda i, _nd=nd: (0,) * _nd, pipeline_mode=pl.Buffered(1))


def _run_layer(layer, x2d, states, weights, fnw, consts, stacked, *, prompt, final_norm):
    tokens = x2d.shape[0]
    if prompt:
        n_seq, seq_len, n_state, n_sub = 1, PROMPT_BLOCK, 1, PROMPT_SUB_BLOCKS
    else:
        n_seq, seq_len, n_state, n_sub = SAMPLE_SEQS, CHUNK, tokens // CHUNK, 1
    tb = n_seq * seq_len
    grid = (tokens // tb,)

    def state_spec(shape):
        nd = len(shape)
        if prompt:
            return pl.BlockSpec((None, 1) + shape, lambda i, _nd=nd: (layer, 0) + (0,) * _nd)
        return pl.BlockSpec((None, n_seq) + shape, lambda i, _nd=nd: (layer, i) + (0,) * _nd)

    in_specs = [pl.BlockSpec((tb, D_MODEL), lambda i: (i, 0))]
    args = [x2d]
    if not prompt:
        in_specs += [state_spec(s) for s in STATE_SHAPES]
        args += list(states)
    in_specs += [_layer_spec(w, layer) for w in weights] + [_const_spec(fnw)] + [_const_spec(c) for c in consts]
    args += list(weights) + [fnw] + list(consts)
    aliases = {}
    for k, buf in enumerate(stacked):
        aliases[len(args)] = 1 + k
        in_specs.append(pl.BlockSpec(memory_space=pl.ANY))
        args.append(buf)

    out_shape = [jax.ShapeDtypeStruct((tokens, D_MODEL), F32)]
    out_shape += [jax.ShapeDtypeStruct((DEPTH, n_state) + s, F32) for s in STATE_SHAPES]
    out_specs = [pl.BlockSpec((tb, D_MODEL), lambda i: (i, 0))] + [state_spec(s) for s in STATE_SHAPES]

    scratch = [pltpu.VMEM((n_seq, HEAD_ROWS + seq_len, D_MODEL), F32),
               pltpu.VMEM((n_seq, HEAD_ROWS + seq_len, SSD_CONV), F32),
               pltpu.VMEM((n_seq, HEAD_ROWS + seq_len, GDN_CONV), F32),
               pltpu.VMEM((tb, D_MODEL), F32),
               pltpu.VMEM((tb, D_MODEL), F32),
               pltpu.VMEM((SSD_N, D_MODEL), F32),
               pltpu.VMEM((tb, D_MODEL), BF16)]
    kern = functools.partial(_layer_kernel, prompt=prompt, n_seq=n_seq, seq_len=seq_len, n_sub=n_sub,
                             final_norm=final_norm, n_alias=len(aliases))
    return pl.pallas_call(
        kern, grid=grid, in_specs=in_specs, out_specs=out_specs, out_shape=out_shape,
        scratch_shapes=scratch, input_output_aliases=aliases,
        compiler_params=pltpu.CompilerParams(dimension_semantics=("arbitrary",),
                                             vmem_limit_bytes=VMEM_LIMIT_BYTES),
    )(*args)


def _expand_matrix(first_lane, n_heads, width):
    rows = jnp.arange(SMALL_W)
    src = rows % 32
    head = src - first_lane
    valid = (rows < 96) & (head >= 0) & (head < n_heads)
    cols = jnp.arange(n_heads * width) // width
    return (valid[:, None] & (head[:, None] == cols[None, :])).astype(BF16)


def _block_tri(tb):
    i = jnp.arange(tb)
    return ((i[:, None] // CHUNK == i[None, :] // CHUNK) & (i[None, :] <= i[:, None])).astype(BF16)


def _split_kernel(w_ref, a_ref, small_ref, gdn_ref, gate_ref):
    a_ref[...] = w_ref[:, 0:IN_DT].astype(BF16)
    gdn_ref[...] = w_ref[:, IN_GDN:IN_BETA].astype(BF16)
    gate_ref[...] = w_ref[:, IN_GATE:IN_END].astype(BF16)
    small = jnp.concatenate([w_ref[:, IN_DT:IN_GDN], w_ref[:, IN_BETA:IN_GATE],
                             jnp.zeros((w_ref.shape[0], SMALL_W - SSD_H - 2 * GDN_HV), F32)], axis=1)
    small_ref[...] = small.astype(BF16)


def _split_w_in(w_in):
    rows = 256
    widths = (IN_DT, SMALL_W, IN_BETA - IN_GDN, IN_END - IN_GATE)
    return pl.pallas_call(
        _split_kernel, grid=(DEPTH, D_MODEL // rows),
        in_specs=[pl.BlockSpec((None, rows, IN_END), lambda l, i: (l, i, 0))],
        out_specs=[pl.BlockSpec((None, rows, w), lambda l, i: (l, i, 0)) for w in widths],
        out_shape=[jax.ShapeDtypeStruct((DEPTH, D_MODEL, w), BF16) for w in widths],
        compiler_params=pltpu.CompilerParams(dimension_semantics=("arbitrary", "arbitrary"),
                                             vmem_limit_bytes=VMEM_LIMIT_BYTES),
    )(w_in)


def _all_layer_weights(norm_w, w_in, pool_w, pool_scale, ssd_conv_w, ssd_conv_b, ssd_dt_bias, ssd_A_log,
                       ssd_D, ssd_norm_w, gdn_conv_w, gdn_dt_bias, gdn_A_log, gdn_norm_w, w_br_pool,
                       w_br_ssd, w_br_gdn, w_out):
    n_pad = SMALL_W - SSD_H - 2 * GDN_HV
    w_a, w_small, w_gdn, w_gate = _split_w_in(w_in)
    pad8 = jnp.zeros((DEPTH, GDN_HV), F32)
    pad = jnp.zeros((DEPTH, n_pad), F32)
    sbias = jnp.concatenate([ssd_dt_bias, pad8, gdn_dt_bias, pad], axis=1)[:, None, :]
    salog = jnp.concatenate([ssd_A_log, pad8, gdn_A_log, pad], axis=1)[:, None, :]
    return [
        norm_w[:, None, :], w_a, w_small, w_gdn, w_gate, pool_w.astype(BF16), pool_scale[:, None, :],
        ssd_conv_w, ssd_conv_b[:, None, :], sbias, salog,
        jnp.repeat(ssd_D, SSD_P, axis=1)[:, None, :], ssd_norm_w[:, None, :],
        gdn_conv_w, jnp.tile(gdn_norm_w, (1, GDN_HV))[:, None, :],
        jnp.stack([w_br_pool, w_br_ssd, w_br_gdn], axis=1).astype(BF16), w_out.astype(BF16),
    ]


def kernel(x_prompt, x_sample, state_pool, state_ssd_conv, state_ssd, state_gdn_conv, state_gdn, norm_w, w_in,
           pool_w, pool_scale, ssd_conv_w, ssd_conv_b, ssd_dt_bias, ssd_A_log, ssd_D, ssd_norm_w, gdn_conv_w,
           gdn_dt_bias, gdn_A_log, gdn_norm_w, w_br_pool, w_br_ssd, w_br_gdn, w_out, final_norm_w):
    bp, seq, _ = x_prompt.shape
    bs, dseq, _ = x_sample.shape
    assert bp == 1 and dseq == CHUNK and seq % PROMPT_BLOCK == 0 and bs % SAMPLE_SEQS == 0
    e_mats = [_expand_matrix(0, SSD_H, SSD_P), _expand_matrix(SSD_H, GDN_HV, GDN_DK),
              _expand_matrix(SSD_H + GDN_HV, GDN_HV, GDN_DK)]
    consts_p = [_block_tri(PROMPT_BLOCK // PROMPT_SUB_BLOCKS)] + e_mats
    consts_s = [_block_tri(SAMPLE_SEQS * CHUNK)] + e_mats
    weights = _all_layer_weights(norm_w, w_in, pool_w, pool_scale, ssd_conv_w, ssd_conv_b, ssd_dt_bias,
                                 ssd_A_log, ssd_D, ssd_norm_w, gdn_conv_w, gdn_dt_bias, gdn_A_log, gdn_norm_w,
                                 w_br_pool, w_br_ssd, w_br_gdn, w_out)
    fnw = final_norm_w[None, :]
    states_s = (state_pool, state_ssd_conv, state_ssd.reshape(DEPTH, bs, D_MODEL, SSD_N),
                state_gdn_conv, state_gdn)

    hp = x_prompt.reshape(seq, D_MODEL)
    hs = x_sample.reshape(bs * dseq, D_MODEL)
    stk_p = [jnp.zeros((DEPTH, 1) + s, F32) for s in STATE_SHAPES]
    stk_s = [jnp.zeros((DEPTH, bs) + s, F32) for s in STATE_SHAPES]
    for l in range(DEPTH):
        last = l == DEPTH - 1
        rp = _run_layer(l, hp, None, weights, fnw, consts_p, stk_p, prompt=True, final_norm=last)
        rs = _run_layer(l, hs, states_s, weights, fnw, consts_s, stk_s, prompt=False, final_norm=last)
        hp, stk_p = rp[0], rp[1:]
        hs, stk_s = rs[0], rs[1:]

    def ssd_out(st):
        return st.reshape(DEPTH, st.shape[1], SSD_H, SSD_P, SSD_N)

    return (hp.reshape(bp, seq, D_MODEL), hs.reshape(bs, dseq, D_MODEL),
            stk_p[0], stk_s[0], stk_p[1], stk_s[1], ssd_out(stk_p[2]), ssd_out(stk_s[2]),
            stk_p[3], stk_s[3], stk_p[4], stk_s[4])
```

```python
import functools

import jax
import jax.numpy as jnp
from jax import lax
from jax.experimental import pallas as pl
from jax.experimental.pallas import tpu as pltpu

F32 = jnp.float32
BF16 = jnp.bfloat16

D_MODEL = 1024
DEPTH = 4
CHUNK = 64
RMS_EPS = 1e-6
POOL_HIST = 15
CONV_HIST = 3
HEAD_ROWS = 16
SSD_H, SSD_P, SSD_N, SSD_G = 16, 64, 128, 2
SSD_CONV = 1536
GDN_HV, GDN_DK, GDN_DV = 8, 128, 128
GDN_CONV = 2048
SMALL_W = 128

W_POOL, W_SSD = 2048, 2560
IN_DT, IN_GDN, IN_BETA, IN_GATE, IN_END = 4608, 4624, 7696, 7712, 10784

PROMPT_BLOCK = 256
PROMPT_SUB_BLOCKS = 1
SAMPLE_SEQS = 2
VMEM_LIMIT_BYTES = 60 * 1024 * 1024

STATE_SHAPES = ((POOL_HIST, D_MODEL), (CONV_HIST, SSD_CONV), (D_MODEL, SSD_N),
                (CONV_HIST, GDN_CONV), (GDN_HV, GDN_DK, GDN_DV))


def _dot(a, b):
    return jnp.dot(a, b, preferred_element_type=F32)


def _dot_nt(a, b):
    return lax.dot_general(a, b, (((1,), (1,)), ((), ())), preferred_element_type=F32)


def _dot_tn(a, b):
    return lax.dot_general(a, b, (((0,), (0,)), ((), ())), preferred_element_type=F32)


def _silu(x):
    return x * jax.nn.sigmoid(x)


def _softplus(x):
    return jnp.maximum(x, 0.0) + jnp.log1p(jnp.exp(-jnp.abs(x)))


def _split3(a):
    hi = a.astype(BF16)
    r = a - hi.astype(F32)
    mid = r.astype(BF16)
    lo = (r - mid.astype(F32)).astype(BF16)
    return hi, mid, lo


def _pack3(a, lane_lt32):
    hi, mid, lo = _split3(a)
    z = jnp.zeros_like(a)
    p = (jnp.where(lane_lt32, hi.astype(F32), z)
         + pltpu.roll(jnp.where(lane_lt32, mid.astype(F32), z), 32, 1)
         + pltpu.roll(jnp.where(lane_lt32, lo.astype(F32), z), 64, 1))
    return p.astype(BF16)


def _causal_conv(buf, s, row, length, w, k_taps):
    acc = buf[s, row:row + length, :] * w[k_taps - 1:k_taps, :]
    for j in range(1, k_taps):
        acc = acc + buf[s, row - j:row - j + length, :] * w[k_taps - 1 - j:k_taps - j, :]
    return acc


def _after(a_bf16, dep):
    u = pltpu.bitcast(dep, jnp.uint32)
    zero = lax.shift_right_logical(lax.shift_right_logical(u, jnp.uint32(16)), jnp.uint32(16))
    bits = pltpu.bitcast(a_bf16, jnp.uint32)
    reps = (bits.shape[0] // zero.shape[0], bits.shape[1] // zero.shape[1])
    return pltpu.bitcast(bits | jnp.tile(zero, reps), BF16)


def _neumann_inverse(xs, eye):
    ps = [eye + x for x in xs]
    xbs = [x.astype(BF16) for x in xs]
    ys = [_dot(xb, xb) for xb in xbs]
    for _ in range(4):
        ybs = [y.astype(BF16) for y in ys]
        rs = [_dot(jnp.concatenate([yb, p.astype(BF16)], axis=0), yb) for yb, p in zip(ybs, ps)]
        ys = [r[:128] for r in rs]
        ps = [p + r[128:] for p, r in zip(ps, rs)]
    return [p + _dot(p.astype(BF16), y.astype(BF16)) for p, y in zip(ps, ys)]


def _layer_kernel(*refs, prompt, n_seq, seq_len, n_sub, final_norm, n_alias):
    tb = n_seq * seq_len
    tbs = tb // n_sub
    n_chunks = tbs // CHUNK
    it = iter(refs)
    x_ref = next(it)
    if not prompt:
        pool_h, sconv_h, ssd_s, gconv_h, gdn_s = (next(it) for _ in range(5))
    (normw, w_a, w_small, w_gdn, w_gate, poolw, poolsc, sconvw, sconvb, sbias, salog, ssd_d, ssdnw, gconvw,
     gdnnw, wbr, wout, fnw, btri, e_dt, e_beta, e_g) = (next(it) for _ in range(22))
    for _ in range(n_alias):
        next(it)
    xo, o_pool, o_sconv, o_ssd, o_gconv, o_gdn = (next(it) for _ in range(6))
    abuf, sbuf, gbuf, yssd, ygdn, st_scr, hn_scr = (next(it) for _ in range(7))

    step = pl.program_id(0)
    if prompt:
        @pl.when(step == 0)
        def _init():
            abuf[:, 0:HEAD_ROWS, :] = jnp.zeros((n_seq, HEAD_ROWS, D_MODEL), F32)
            sbuf[:, 0:HEAD_ROWS, :] = jnp.zeros((n_seq, HEAD_ROWS, SSD_CONV), F32)
            gbuf[:, 0:HEAD_ROWS, :] = jnp.zeros((n_seq, HEAD_ROWS, GDN_CONV), F32)
            st_scr[...] = jnp.zeros(st_scr.shape, F32)
            o_gdn[...] = jnp.zeros(o_gdn.shape, F32)
    else:
        abuf[:, 0:HEAD_ROWS, :] = jnp.zeros((n_seq, HEAD_ROWS, D_MODEL), F32)
        sbuf[:, 0:HEAD_ROWS, :] = jnp.zeros((n_seq, HEAD_ROWS, SSD_CONV), F32)
        gbuf[:, 0:HEAD_ROWS, :] = jnp.zeros((n_seq, HEAD_ROWS, GDN_CONV), F32)
        abuf[:, HEAD_ROWS - POOL_HIST:HEAD_ROWS, :] = pool_h[...]
        sbuf[:, HEAD_ROWS - CONV_HIST:HEAD_ROWS, :] = sconv_h[...]
        gbuf[:, HEAD_ROWS - CONV_HIST:HEAD_ROWS, :] = gconv_h[...]

    row16 = lax.broadcasted_iota(jnp.int32, (HEAD_ROWS, 256), 0).astype(F32)
    lane = lax.broadcasted_iota(jnp.int32, (1, SMALL_W), 1)
    is_decay = (lane < SSD_H) | ((lane >= SSD_H + GDN_HV) & (lane < SSD_H + 2 * GDN_HV))
    lane_lt32 = lane < 32
    q_idx = lax.broadcasted_iota(jnp.int32, (CHUNK, D_MODEL), 0)
    s_idx = lax.broadcasted_iota(jnp.int32, (CHUNK, D_MODEL), 1) & (CHUNK - 1)
    causal = s_idx <= q_idx
    diag_sel = (s_idx == q_idx).astype(F32)
    ri = lax.broadcasted_iota(jnp.int32, (128, 128), 0)
    ci = lax.broadcasted_iota(jnp.int32, (128, 128), 1)
    same = (ri >> 6) == (ci >> 6)
    incl = same & ((ci & 63) <= (ri & 63))
    strict = same & ((ci & 63) < (ri & 63))
    eye = (ri == ci).astype(F32)
    q64 = lax.broadcasted_iota(jnp.int32, (CHUNK, 128), 0)
    l64 = lax.broadcasted_iota(jnp.int32, (CHUNK, 128), 1)
    diag_sel2 = ((l64 & 63) == q64).astype(F32)
    lo_half64 = l64 < CHUNK

    def l2n(t):
        return t * lax.rsqrt(jnp.sum(t * t, axis=-1, keepdims=True) + 1e-6)

    def sub_block(sub):
        r0 = sub * tbs
        rows = slice(r0, r0 + tbs)
        if prompt:
            seqs = [(0, r0, 0, tbs)]
        else:
            seqs = [(s, 0, s * seq_len, seq_len) for s in range(n_seq)]
        chunks_per_seq = seqs[0][3] // CHUNK

        x = x_ref[rows, :]
        hn_scr[rows, :] = (x * lax.rsqrt(jnp.mean(x * x, axis=-1, keepdims=True) + RMS_EPS)
                           * normw[...]).astype(BF16)
        pp = _dot(hn_scr[rows, :], w_a[:, 0:W_POOL])
        sm = _dot(hn_scr[rows, :], w_small[...])

        a_in, z_pool = pp[:, :D_MODEL], pp[:, D_MODEL:]
        y_pool_parts = []
        for s, b0, x0, ln in seqs:
            abuf[s, b0 + HEAD_ROWS:b0 + HEAD_ROWS + ln, :] = a_in[x0:x0 + ln]
            full = abuf[s, b0:b0 + HEAD_ROWS + ln, :]
            o_pool[s] = full[ln + HEAD_ROWS - POOL_HIST:, :]
            ys = []
            for g in range(4):
                win = 2 << g
                fg = full[:, 256 * g:256 * (g + 1)]
                acc = fg
                for k in range(g + 1):
                    acc = acc + pltpu.roll(acc, 1 << k, 0)
                wsum = acc[HEAD_ROWS:]
                cur = fg[HEAD_ROWS:]
                inv_w = 1.0 / win
                if prompt and sub == 0:
                    fac = jnp.where(step == 0, 1.0 / jnp.minimum(row16 + 1.0, float(win)), inv_w)
                    d = jnp.concatenate([wsum[:HEAD_ROWS] * fac - cur[:HEAD_ROWS],
                                         wsum[HEAD_ROWS:] * inv_w - cur[HEAD_ROWS:]], axis=0)
                else:
                    d = wsum * inv_w - cur
                ys.append(_dot(d.astype(BF16), poolw[g]))
            y_pool_parts.append(jnp.concatenate(ys, axis=1))
        y_pool = jnp.concatenate(y_pool_parts, axis=0) if len(seqs) > 1 else y_pool_parts[0]
        y_pool = (y_pool * poolsc[...] * _silu(z_pool)).astype(BF16)

        sp = _softplus(sm + sbias[...])
        sg = jax.nn.sigmoid(sm)
        av = sp * jnp.where(is_decay, -jnp.exp(salog[...]), 0.0)
        a_hi, a_mid, a_lo = _split3(av)
        bt = btri[...]
        acum = _dot(bt, a_hi) + _dot(bt, a_mid) + _dot(bt, a_lo)
        pk_ac = _pack3(acum, lane_lt32)
        dt_exp = _dot(_pack3(sp, lane_lt32), e_dt[...])
        acum_exp = _dot(pk_ac, e_dt[...])
        beta128 = _dot(_pack3(sg, lane_lt32), e_beta[...])
        gc128 = _dot(pk_ac, e_g[...])

        ps = _dot(hn_scr[rows, :], w_a[:, W_POOL:W_POOL + W_SSD])
        u_ssd = jnp.concatenate([ps[:, :D_MODEL], ps[:, 2 * D_MODEL:]], axis=1)
        z_ssd = ps[:, D_MODEL:2 * D_MODEL]
        gate_mid = 3 * D_MODEL // 2
        gate_a = _dot(_after(hn_scr[rows, :], ps[0:8, 0:128]), w_gate[:, 0:gate_mid])
        acts = []
        for s, b0, x0, ln in seqs:
            sbuf[s, b0 + HEAD_ROWS:b0 + HEAD_ROWS + ln, :] = u_ssd[x0:x0 + ln]
            o_sconv[s] = u_ssd[x0 + ln - CONV_HIST:x0 + ln]
            acts.append(_silu(_causal_conv(sbuf, s, b0 + HEAD_ROWS, ln, sconvw[...], 4) + sconvb[...]))
        xbc = jnp.concatenate(acts, axis=0) if len(seqs) > 1 else acts[0]
        xc, bm, cm_ = xbc[:, :D_MODEL], xbc[:, D_MODEL:D_MODEL + 256], xbc[:, D_MODEL + 256:]
        br_pool = _dot(_after(y_pool, xc[0:8, 0:128]), wbr[0])

        cr = [slice(c * CHUNK, (c + 1) * CHUNK) for c in range(n_chunks)]
        acs = [acum_exp[r] for r in cr]
        a_rows = [jnp.sum(ac * diag_sel, axis=0, keepdims=True) for ac in acs]
        l_alls = [jnp.where(causal, jnp.exp(ac - a_row), 0.0) for ac, a_row in zip(acs, a_rows)]
        bcs = [bm[r].astype(BF16) for r in cr]
        ccs = [cm_[r].astype(BF16) for r in cr]
        cbts = [jnp.concatenate(
            [_dot_nt(cc[:, 128 * g:128 * (g + 1)], jnp.concatenate([bc[:, 128 * g:128 * (g + 1)]] * 8, axis=0))
             for g in range(SSD_G)], axis=1) for bc, cc in zip(bcs, ccs)]
        m_alls = [(cbt * l_all).astype(BF16) for cbt, l_all in zip(cbts, l_alls)]
        xdts = [xc[r] * dt_exp[r] for r in cr]
        zb = jnp.zeros((CHUNK, 128), BF16)
        y_intras = []
        for m_all, xdt in zip(m_alls, xdts):
            xdtb = xdt.astype(BF16)
            y_parts = []
            for j in range(8):
                xp = xdtb[:, 128 * j:128 * (j + 1)]
                rhs = jnp.concatenate([jnp.where(lo_half64, xp, zb), jnp.where(lo_half64, zb, xp)], axis=0)
                y_parts.append(_dot(m_all[:, 128 * j:128 * (j + 1)], rhs))
            y_intras.append(jnp.concatenate(y_parts, axis=1))
        a_lasts = [ac[CHUNK - 1:CHUNK, :] for ac in acs]
        xds = [(xdt * jnp.exp(a_last - ac)).astype(BF16) for xdt, a_last, ac in zip(xdts, a_lasts, acs)]
        upds = [jnp.concatenate(
            [_dot_tn(bc[:, 128 * g:128 * (g + 1)], xd[:, 512 * g:512 * (g + 1)]) for g in range(SSD_G)], axis=1)
            for bc, xd in zip(bcs, xds)]

        st = None
        for c in range(n_chunks):
            r = cr[c]
            s = c // chunks_per_seq
            if c % chunks_per_seq == 0:
                st = st_scr[...] if prompt else ssd_s[s].T
            stb = st.astype(BF16)
            y = y_intras[c] + jnp.exp(acs[c]) * jnp.concatenate(
                [_dot(ccs[c][:, 128 * g:128 * (g + 1)], stb[:, 512 * g:512 * (g + 1)])
                 for g in range(SSD_G)], axis=1)
            st = jnp.exp(a_lasts[c]) * st + upds[c]
            if (c + 1) % chunks_per_seq == 0:
                if prompt:
                    st_scr[...] = st
                else:
                    o_ssd[s] = st.T
            y = y + ssd_d[...] * xc[r]
            yz = y * _silu(z_ssd[r])
            yn = jnp.concatenate(
                [yz[:, 512 * g:512 * (g + 1)]
                 * lax.rsqrt(jnp.mean(yz[:, 512 * g:512 * (g + 1)] ** 2, axis=-1, keepdims=True) + RMS_EPS)
                 for g in range(SSD_G)], axis=1)
            yssd[r0 + c * CHUNK:r0 + (c + 1) * CHUNK, :] = yn * ssdnw[...]
        y_ssd = yssd[rows, :].astype(BF16)

        pg = _dot(hn_scr[rows, :], w_gdn[...])
        u_gdn, z_gdn = pg[:, :GDN_CONV], pg[:, GDN_CONV:]
        gate_b = _dot(_after(hn_scr[rows, :], pg[0:8, 0:128]), w_gate[:, gate_mid:])
        acts = []
        for s, b0, x0, ln in seqs:
            gbuf[s, b0 + HEAD_ROWS:b0 + HEAD_ROWS + ln, :] = u_gdn[x0:x0 + ln]
            o_gconv[s] = u_gdn[x0 + ln - CONV_HIST:x0 + ln]
            acts.append(_silu(_causal_conv(gbuf, s, b0 + HEAD_ROWS, ln, gconvw[...], 4)))
        qkv = jnp.concatenate(acts, axis=0) if len(seqs) > 1 else acts[0]

        qn = [l2n(qkv[:, 128 * j:128 * (j + 1)]) * (GDN_DK ** -0.5) for j in range(4)]
        kn = [l2n(qkv[:, 512 + 128 * j:512 + 128 * (j + 1)]) for j in range(4)]
        vc = qkv[:, 1024:]

        combos = [(c, j) for c in range(n_chunks) for j in range(4)]
        pre = {}
        lps = []
        for c, j in combos:
            r = slice(c * CHUNK, (c + 1) * CHUNK)
            h0, h1 = 2 * j, 2 * j + 1
            kj, qj = kn[j][r], qn[j][r]
            g0 = gc128[r, 128 * h0:128 * (h0 + 1)]
            g1 = gc128[r, 128 * h1:128 * (h1 + 1)]
            kst = jnp.concatenate([kj, kj], axis=0)
            qst = jnp.concatenate([qj, qj], axis=0)
            bst = jnp.concatenate([beta128[r, 128 * h0:128 * (h0 + 1)],
                                   beta128[r, 128 * h1:128 * (h1 + 1)]], axis=0)
            gst = jnp.concatenate([g0, g1], axis=0)
            vst = jnp.concatenate([vc[r, 128 * h0:128 * (h0 + 1)], vc[r, 128 * h1:128 * (h1 + 1)]], axis=0)
            g_row = jnp.sum(jnp.where(lo_half64, g0, g1) * diag_sel2, axis=0, keepdims=True)
            dec = jnp.exp(gst - g_row)
            eg = jnp.exp(gst)
            kq = _dot_nt(jnp.concatenate([kj, qj], axis=0).astype(BF16), kst.astype(BF16))
            kk2 = jnp.concatenate([kq[:CHUNK], kq[:CHUNK]], axis=0)
            qk2 = jnp.concatenate([kq[CHUNK:], kq[CHUNK:]], axis=0)
            lps.append(jnp.where(strict, kk2 * bst * dec, 0.0))
            gl0, gl1 = g0[CHUNK - 1:CHUNK, :], g1[CHUNK - 1:CHUNK, :]
            pre[(c, j)] = dict(
                ap=jnp.where(incl, qk2 * dec, 0.0).astype(BF16),
                rhs=jnp.concatenate([vst * bst, kst * bst * eg], axis=1).astype(BF16),
                qg=(qst * eg).astype(BF16),
                kjb=kj.astype(BF16),
                e_st=jnp.exp(jnp.concatenate([gl0 - g0, gl1 - g1], axis=0)),
                dl0=jnp.exp(gl0), dl1=jnp.exp(gl1))
        t_invs = _neumann_inverse([-lp for lp in lps], eye)
        for (c, j), t_inv in zip(combos, t_invs):
            uw = _dot(t_inv.astype(BF16), pre[(c, j)]["rhs"])
            pre[(c, j)]["u"] = uw[:, :128]
            pre[(c, j)]["w"] = uw[:, 128:].astype(BF16)

        gs = [None] * GDN_HV
        for c in range(n_chunks):
            s = c // chunks_per_seq
            if c % chunks_per_seq == 0:
                for h in range(GDN_HV):
                    gs[h] = o_gdn[0, h] if prompt else gdn_s[s, h]
            rr = []
            for j in range(4):
                p_ = pre[(c, j)]
                for hh in range(2):
                    half = slice(hh * CHUNK, (hh + 1) * CHUNK)
                    rr.append(_dot(jnp.concatenate([p_["w"][half], p_["qg"][half]], axis=0),
                                   gs[2 * j + hh].astype(BF16)))
            vnews = [pre[(c, j)]["u"] - jnp.concatenate([rr[2 * j][:CHUNK], rr[2 * j + 1][:CHUNK]], axis=0)
                     for j in range(4)]
            upds = []
            for j in range(4):
                vn_e = vnews[j] * pre[(c, j)]["e_st"]
                upds.append(_dot_tn(pre[(c, j)]["kjb"],
                                    jnp.concatenate([vn_e[:CHUNK], vn_e[CHUNK:]], axis=1).astype(BF16)))
            for j in range(4):
                h0, h1 = 2 * j, 2 * j + 1
                gs[h0] = pre[(c, j)]["dl0"] * gs[h0] + upds[j][:, :128]
                gs[h1] = pre[(c, j)]["dl1"] * gs[h1] + upds[j][:, 128:]
            yr = slice(r0 + c * CHUNK, r0 + (c + 1) * CHUNK)
            for j in range(4):
                h0, h1 = 2 * j, 2 * j + 1
                o_st = (jnp.concatenate([rr[2 * j][CHUNK:], rr[2 * j + 1][CHUNK:]], axis=0)
                        + _dot(pre[(c, j)]["ap"], vnews[j].astype(BF16)))
                ygdn[yr, 128 * h0:128 * (h0 + 1)] = o_st[:CHUNK]
                ygdn[yr, 128 * h1:128 * (h1 + 1)] = o_st[CHUNK:]
            if (c + 1) % chunks_per_seq == 0:
                for h in range(GDN_HV):
                    o_gdn[s, h] = gs[h]
        og = ygdn[rows, :]
        og = jnp.concatenate(
            [og[:, 128 * h:128 * (h + 1)]
             * lax.rsqrt(jnp.mean(og[:, 128 * h:128 * (h + 1)] ** 2, axis=-1, keepdims=True) + RMS_EPS)
             for h in range(GDN_HV)], axis=1)
        y_gdn = (og * gdnnw[...] * _silu(z_gdn)).astype(BF16)

        gates = jax.nn.sigmoid(jnp.concatenate([gate_a, gate_b], axis=1))
        merged = (gates[:, :D_MODEL] * br_pool
                  + gates[:, D_MODEL:2 * D_MODEL] * _dot(y_ssd, wbr[1])
                  + gates[:, 2 * D_MODEL:] * _dot(y_gdn, wbr[2]))
        xn = x + _dot(merged.astype(BF16), wout[...])
        if final_norm:
            xn = xn * lax.rsqrt(jnp.mean(xn * xn, axis=-1, keepdims=True) + RMS_EPS) * fnw[...]
        xo[rows, :] = xn

    for sub in range(n_sub):
        sub_block(sub)

    if prompt:
        abuf[0, 0:HEAD_ROWS, :] = abuf[0, tb:tb + HEAD_ROWS, :]
        sbuf[0, 0:HEAD_ROWS, :] = sbuf[0, tb:tb + HEAD_ROWS, :]
        gbuf[0, 0:HEAD_ROWS, :] = gbuf[0, tb:tb + HEAD_ROWS, :]

        @pl.when(step == pl.num_programs(0) - 1)
        def _final_state():
            o_ssd[0] = st_scr[...].T


def _layer_spec(arr, layer):
    nd = arr.ndim - 1
    return pl.BlockSpec((None,) + arr.shape[1:], lambda i, _l=layer, _nd=nd: (_l,) + (0,) * _nd,
                        pipeline_mode=pl.Buffered(1))


def _const_spec(arr):
    nd = arr.ndim
    return pl.BlockSpec(arr.shape, lambda i, _nd=nd: (0,) * _nd, pipeline_mode=pl.Buffered(1))


def _run_layer(layer, x2d, states, weights, fnw, consts, stacked, *, prompt, final_norm):
    tokens = x2d.shape[0]
    if prompt:
        n_seq, seq_len, n_state, n_sub = 1, PROMPT_BLOCK, 1, PROMPT_SUB_BLOCKS
    else:
        n_seq, seq_len, n_state, n_sub = SAMPLE_SEQS, CHUNK, tokens // CHUNK, 1
    tb = n_seq * seq_len
    grid = (tokens // tb,)

    def state_spec(shape):
        nd = len(shape)
        if prompt:
            return pl.BlockSpec((None, 1) + shape, lambda i, _nd=nd: (layer, 0) + (0,) * _nd)
        return pl.BlockSpec((None, n_seq) + shape, lambda i, _nd=nd: (layer, i) + (0,) * _nd)

    in_specs = [pl.BlockSpec((tb, D_MODEL), lambda i: (i, 0))]
    args = [x2d]
    if not prompt:
        in_specs += [state_spec(s) for s in STATE_SHAPES]
        args += list(states)
    in_specs += [_layer_spec(w, layer) for w in weights] + [_const_spec(fnw)] + [_const_spec(c) for c in consts]
    args += list(weights) + [fnw] + list(consts)
    aliases = {}
    for k, buf in enumerate(stacked):
        aliases[len(args)] = 1 + k
        in_specs.append(pl.BlockSpec(memory_space=pl.ANY))
        args.append(buf)

    out_shape = [jax.ShapeDtypeStruct((tokens, D_MODEL), F32)]
    out_shape += [jax.ShapeDtypeStruct((DEPTH, n_state) + s, F32) for s in STATE_SHAPES]
    out_specs = [pl.BlockSpec((tb, D_MODEL), lambda i: (i, 0))] + [state_spec(s) for s in STATE_SHAPES]

    scratch = [pltpu.VMEM((n_seq, HEAD_ROWS + seq_len, D_MODEL), F32),
               pltpu.VMEM((n_seq, HEAD_ROWS + seq_len, SSD_CONV), F32),
               pltpu.VMEM((n_seq, HEAD_ROWS + seq_len, GDN_CONV), F32),
               pltpu.VMEM((tb, D_MODEL), F32),
               pltpu.VMEM((tb, D_MODEL), F32),
               pltpu.VMEM((SSD_N, D_MODEL), F32),
               pltpu.VMEM((tb, D_MODEL), BF16)]
    kern = functools.partial(_layer_kernel, prompt=prompt, n_seq=n_seq, seq_len=seq_len, n_sub=n_sub,
                             final_norm=final_norm, n_alias=len(aliases))
    return pl.pallas_call(
        kern, grid=grid, in_specs=in_specs, out_specs=out_specs, out_shape=out_shape,
        scratch_shapes=scratch, input_output_aliases=aliases,
        compiler_params=pltpu.CompilerParams(dimension_semantics=("arbitrary",),
                                             vmem_limit_bytes=VMEM_LIMIT_BYTES),
    )(*args)


def _expand_matrix(first_lane, n_heads, width):
    rows = jnp.arange(SMALL_W)
    src = rows % 32
    head = src - first_lane
    valid = (rows < 96) & (head >= 0) & (head < n_heads)
    cols = jnp.arange(n_heads * width) // width
    return (valid[:, None] & (head[:, None] == cols[None, :])).astype(BF16)


def _block_tri(tb):
    i = jnp.arange(tb)
    return ((i[:, None] // CHUNK == i[None, :] // CHUNK) & (i[None, :] <= i[:, None])).astype(BF16)


def _all_layer_weights(norm_w, w_in, pool_w, pool_scale, ssd_conv_w, ssd_conv_b, ssd_dt_bias, ssd_A_log,
                       ssd_D, ssd_norm_w, gdn_conv_w, gdn_dt_bias, gdn_A_log, gdn_norm_w, w_br_pool,
                       w_br_ssd, w_br_gdn, w_out):
    n_pad = SMALL_W - SSD_H - 2 * GDN_HV
    w_a = w_in[:, :, 0:IN_DT].astype(BF16)
    w_gdn = w_in[:, :, IN_GDN:IN_BETA].astype(BF16)
    w_gate = w_in[:, :, IN_GATE:IN_END].astype(BF16)
    w_small = jnp.concatenate([w_in[:, :, IN_DT:IN_GDN], w_in[:, :, IN_BETA:IN_GATE],
                               jnp.zeros((DEPTH, D_MODEL, n_pad), F32)], axis=2).astype(BF16)
    pad8 = jnp.zeros((DEPTH, GDN_HV), F32)
    pad = jnp.zeros((DEPTH, n_pad), F32)
    sbias = jnp.concatenate([ssd_dt_bias, pad8, gdn_dt_bias, pad], axis=1)[:, None, :]
    salog = jnp.concatenate([ssd_A_log, pad8, gdn_A_log, pad], axis=1)[:, None, :]
    return [
        norm_w[:, None, :], w_a, w_small, w_gdn, w_gate, pool_w.astype(BF16), pool_scale[:, None, :],
        ssd_conv_w, ssd_conv_b[:, None, :], sbias, salog,
        jnp.repeat(ssd_D, SSD_P, axis=1)[:, None, :], ssd_norm_w[:, None, :],
        gdn_conv_w, jnp.tile(gdn_norm_w, (1, GDN_HV))[:, None, :],
        jnp.stack([w_br_pool, w_br_ssd, w_br_gdn], axis=1).astype(BF16), w_out.astype(BF16),
    ]


def kernel(x_prompt, x_sample, state_pool, state_ssd_conv, state_ssd, state_gdn_conv, state_gdn, norm_w, w_in,
           pool_w, pool_scale, ssd_conv_w, ssd_conv_b, ssd_dt_bias, ssd_A_log, ssd_D, ssd_norm_w, gdn_conv_w,
           gdn_dt_bias, gdn_A_log, gdn_norm_w, w_br_pool, w_br_ssd, w_br_gdn, w_out, final_norm_w):
    bp, seq, _ = x_prompt.shape
    bs, dseq, _ = x_sample.shape
    assert bp == 1 and dseq == CHUNK and seq % PROMPT_BLOCK == 0 and bs % SAMPLE_SEQS == 0
    e_mats = [_expand_matrix(0, SSD_H, SSD_P), _expand_matrix(SSD_H, GDN_HV, GDN_DK),
              _expand_matrix(SSD_H + GDN_HV, GDN_HV, GDN_DK)]
    consts_p = [_block_tri(PROMPT_BLOCK // PROMPT_SUB_BLOCKS)] + e_mats
    consts_s = [_block_tri(SAMPLE_SEQS * CHUNK)] + e_mats
    weights = _all_layer_weights(norm_w, w_in, pool_w, pool_scale, ssd_conv_w, ssd_conv_b, ssd_dt_bias,
                                 ssd_A_log, ssd_D, ssd_norm_w, gdn_conv_w, gdn_dt_bias, gdn_A_log, gdn_norm_w,
                                 w_br_pool, w_br_ssd, w_br_gdn, w_out)
    fnw = final_norm_w[None, :]
    states_s = (state_pool, state_ssd_conv, state_ssd.reshape(DEPTH, bs, D_MODEL, SSD_N),
                state_gdn_conv, state_gdn)

    hp = x_prompt.reshape(seq, D_MODEL)
    hs = x_sample.reshape(bs * dseq, D_MODEL)
    stk_p = [jnp.zeros((DEPTH, 1) + s, F32) for s in STATE_SHAPES]
    stk_s = [jnp.zeros((DEPTH, bs) + s, F32) for s in STATE_SHAPES]
    for l in range(DEPTH):
        last = l == DEPTH - 1
        rp = _run_layer(l, hp, None, weights, fnw, consts_p, stk_p, prompt=True, final_norm=last)
        rs = _run_layer(l, hs, states_s, weights, fnw, consts_s, stk_s, prompt=False, final_norm=last)
        hp, stk_p = rp[0], rp[1:]
        hs, stk_s = rs[0], rs[1:]

    def ssd_out(st):
        return st.reshape(DEPTH, st.shape[1], SSD_H, SSD_P, SSD_N)

    return (hp.reshape(bp, seq, D_MODEL), hs.reshape(bs, dseq, D_MODEL),
            stk_p[0], stk_s[0], stk_p[1], stk_s[1], ssd_out(stk_p[2]), ssd_out(stk_s[2]),
            stk_p[3], stk_s[3], stk_p[4], stk_s[4])
```

```python
import functools

import jax
import jax.numpy as jnp
from jax import lax
from jax.experimental import pallas as pl
from jax.experimental.pallas import tpu as pltpu

F32 = jnp.float32
BF16 = jnp.bfloat16

D_MODEL = 1024
DEPTH = 4
CHUNK = 64
RMS_EPS = 1e-6
POOL_HIST = 15
CONV_HIST = 3
HEAD_ROWS = 16
SSD_H, SSD_P, SSD_N, SSD_G = 16, 64, 128, 2
SSD_CONV = 1536
GDN_HV, GDN_DK, GDN_DV = 8, 128, 128
GDN_CONV = 2048
SMALL_W = 128

W_POOL, W_SSD = 2048, 2560
IN_DT, IN_GDN, IN_BETA, IN_GATE, IN_END = 4608, 4624, 7696, 7712, 10784

PROMPT_BLOCK = 256
PROMPT_SUB_BLOCKS = 1
SAMPLE_SEQS = 2
VMEM_LIMIT_BYTES = 60 * 1024 * 1024

STATE_SHAPES = ((POOL_HIST, D_MODEL), (CONV_HIST, SSD_CONV), (D_MODEL, SSD_N),
                (CONV_HIST, GDN_CONV), (GDN_HV, GDN_DK, GDN_DV))


def _dot(a, b):
    return jnp.dot(a, b, preferred_element_type=F32)


def _dot_nt(a, b):
    return lax.dot_general(a, b, (((1,), (1,)), ((), ())), preferred_element_type=F32)


def _dot_tn(a, b):
    return lax.dot_general(a, b, (((0,), (0,)), ((), ())), preferred_element_type=F32)


def _silu(x):
    return x * jax.nn.sigmoid(x)


def _softplus(x):
    return jnp.maximum(x, 0.0) + jnp.log1p(jnp.exp(-jnp.abs(x)))


def _split3(a):
    hi = a.astype(BF16)
    r = a - hi.astype(F32)
    mid = r.astype(BF16)
    lo = (r - mid.astype(F32)).astype(BF16)
    return hi, mid, lo


def _pack3(a, lane_lt32):
    hi, mid, lo = _split3(a)
    z = jnp.zeros_like(a)
    p = (jnp.where(lane_lt32, hi.astype(F32), z)
         + pltpu.roll(jnp.where(lane_lt32, mid.astype(F32), z), 32, 1)
         + pltpu.roll(jnp.where(lane_lt32, lo.astype(F32), z), 64, 1))
    return p.astype(BF16)


def _causal_conv(buf, s, row, length, w, k_taps):
    acc = buf[s, row:row + length, :] * w[k_taps - 1:k_taps, :]
    for j in range(1, k_taps):
        acc = acc + buf[s, row - j:row - j + length, :] * w[k_taps - 1 - j:k_taps - j, :]
    return acc


def _after(a_bf16, dep):
    u = pltpu.bitcast(dep, jnp.uint32)
    zero = lax.shift_right_logical(lax.shift_right_logical(u, jnp.uint32(16)), jnp.uint32(16))
    bits = pltpu.bitcast(a_bf16, jnp.uint32)
    reps = (bits.shape[0] // zero.shape[0], bits.shape[1] // zero.shape[1])
    return pltpu.bitcast(bits | jnp.tile(zero, reps), BF16)


def _neumann_inverse(xs, eye):
    ps = [eye + x for x in xs]
    xbs = [x.astype(BF16) for x in xs]
    ys = [_dot(xb, xb) for xb in xbs]
    for _ in range(4):
        ybs = [y.astype(BF16) for y in ys]
        rs = [_dot(jnp.concatenate([yb, p.astype(BF16)], axis=0), yb) for yb, p in zip(ybs, ps)]
        ys = [r[:128] for r in rs]
        ps = [p + r[128:] for p, r in zip(ps, rs)]
    return [p + _dot(p.astype(BF16), y.astype(BF16)) for p, y in zip(ps, ys)]


def _layer_kernel(*refs, prompt, n_seq, seq_len, n_sub, final_norm, n_alias):
    tb = n_seq * seq_len
    tbs = tb // n_sub
    n_chunks = tbs // CHUNK
    it = iter(refs)
    x_ref = next(it)
    xnext_ref = next(it)
    if not prompt:
        pool_h, sconv_h, ssd_s, gconv_h, gdn_s = (next(it) for _ in range(5))
    (normw, w_a, w_small, w_gdn, w_gate, poolw, poolsc, sconvw, sconvb, sbias, salog, ssd_d, ssdnw, gconvw,
     gdnnw, wbr, wout, fnw, btri, e_dt, e_beta, e_g) = (next(it) for _ in range(22))
    for _ in range(n_alias):
        next(it)
    xo, o_pool, o_sconv, o_ssd, o_gconv, o_gdn = (next(it) for _ in range(6))
    abuf, sbuf, gbuf, yssd, ygdn, st_scr, hn_scr = (next(it) for _ in range(7))

    step = pl.program_id(0)
    if prompt:
        @pl.when(step == 0)
        def _init():
            abuf[:, 0:HEAD_ROWS, :] = jnp.zeros((n_seq, HEAD_ROWS, D_MODEL), F32)
            sbuf[:, 0:HEAD_ROWS, :] = jnp.zeros((n_seq, HEAD_ROWS, SSD_CONV), F32)
            gbuf[:, 0:HEAD_ROWS, :] = jnp.zeros((n_seq, HEAD_ROWS, GDN_CONV), F32)
            st_scr[...] = jnp.zeros(st_scr.shape, F32)
            o_gdn[...] = jnp.zeros(o_gdn.shape, F32)
    else:
        abuf[:, 0:HEAD_ROWS, :] = jnp.zeros((n_seq, HEAD_ROWS, D_MODEL), F32)
        sbuf[:, 0:HEAD_ROWS, :] = jnp.zeros((n_seq, HEAD_ROWS, SSD_CONV), F32)
        gbuf[:, 0:HEAD_ROWS, :] = jnp.zeros((n_seq, HEAD_ROWS, GDN_CONV), F32)
        abuf[:, HEAD_ROWS - POOL_HIST:HEAD_ROWS, :] = pool_h[...]
        sbuf[:, HEAD_ROWS - CONV_HIST:HEAD_ROWS, :] = sconv_h[...]
        gbuf[:, HEAD_ROWS - CONV_HIST:HEAD_ROWS, :] = gconv_h[...]

    row16 = lax.broadcasted_iota(jnp.int32, (HEAD_ROWS, 256), 0).astype(F32)
    lane = lax.broadcasted_iota(jnp.int32, (1, SMALL_W), 1)
    is_decay = (lane < SSD_H) | ((lane >= SSD_H + GDN_HV) & (lane < SSD_H + 2 * GDN_HV))
    lane_lt32 = lane < 32
    q_idx = lax.broadcasted_iota(jnp.int32, (CHUNK, D_MODEL), 0)
    s_idx = lax.broadcasted_iota(jnp.int32, (CHUNK, D_MODEL), 1) & (CHUNK - 1)
    causal = s_idx <= q_idx
    diag_sel = (s_idx == q_idx).astype(F32)
    ri = lax.broadcasted_iota(jnp.int32, (128, 128), 0)
    ci = lax.broadcasted_iota(jnp.int32, (128, 128), 1)
    same = (ri >> 6) == (ci >> 6)
    incl = same & ((ci & 63) <= (ri & 63))
    strict = same & ((ci & 63) < (ri & 63))
    eye = (ri == ci).astype(F32)
    q64 = lax.broadcasted_iota(jnp.int32, (CHUNK, 128), 0)
    l64 = lax.broadcasted_iota(jnp.int32, (CHUNK, 128), 1)
    diag_sel2 = ((l64 & 63) == q64).astype(F32)
    lo_half64 = l64 < CHUNK

    def l2n(t):
        return t * lax.rsqrt(jnp.sum(t * t, axis=-1, keepdims=True) + 1e-6)

    def normed(xv):
        return (xv * lax.rsqrt(jnp.mean(xv * xv, axis=-1, keepdims=True) + RMS_EPS) * normw[...]).astype(BF16)

    slot = lax.rem(step, 2)

    @pl.when(step == 0)
    def _first_norm():
        hn_scr[0] = normed(x_ref[...])

    def sub_block(sub):
        r0 = sub * tbs
        rows = slice(r0, r0 + tbs)
        if prompt:
            seqs = [(0, r0, 0, tbs)]
        else:
            seqs = [(s, 0, s * seq_len, seq_len) for s in range(n_seq)]
        chunks_per_seq = seqs[0][3] // CHUNK

        x = x_ref[rows, :]
        pp = _dot(hn_scr[slot, rows, :], w_a[:, 0:W_POOL])
        sm = _dot(hn_scr[slot, rows, :], w_small[...])

        a_in, z_pool = pp[:, :D_MODEL], pp[:, D_MODEL:]
        y_pool_parts = []
        for s, b0, x0, ln in seqs:
            abuf[s, b0 + HEAD_ROWS:b0 + HEAD_ROWS + ln, :] = a_in[x0:x0 + ln]
            full = abuf[s, b0:b0 + HEAD_ROWS + ln, :]
            o_pool[s] = full[ln + HEAD_ROWS - POOL_HIST:, :]
            ys = []
            for g in range(4):
                win = 2 << g
                fg = full[:, 256 * g:256 * (g + 1)]
                acc = fg
                for k in range(g + 1):
                    acc = acc + pltpu.roll(acc, 1 << k, 0)
                wsum = acc[HEAD_ROWS:]
                cur = fg[HEAD_ROWS:]
                inv_w = 1.0 / win
                if prompt and sub == 0:
                    fac = jnp.where(step == 0, 1.0 / jnp.minimum(row16 + 1.0, float(win)), inv_w)
                    d = jnp.concatenate([wsum[:HEAD_ROWS] * fac - cur[:HEAD_ROWS],
                                         wsum[HEAD_ROWS:] * inv_w - cur[HEAD_ROWS:]], axis=0)
                else:
                    d = wsum * inv_w - cur
                ys.append(_dot(d.astype(BF16), poolw[g]))
            y_pool_parts.append(jnp.concatenate(ys, axis=1))
        y_pool = jnp.concatenate(y_pool_parts, axis=0) if len(seqs) > 1 else y_pool_parts[0]
        y_pool = (y_pool * poolsc[...] * _silu(z_pool)).astype(BF16)

        sp = _softplus(sm + sbias[...])
        sg = jax.nn.sigmoid(sm)
        av = sp * jnp.where(is_decay, -jnp.exp(salog[...]), 0.0)
        a_hi, a_mid, a_lo = _split3(av)
        bt = btri[...]
        acum = _dot(bt, a_hi) + _dot(bt, a_mid) + _dot(bt, a_lo)
        pk_ac = _pack3(acum, lane_lt32)
        dt_exp = _dot(_pack3(sp, lane_lt32), e_dt[...])
        acum_exp = _dot(pk_ac, e_dt[...])
        beta128 = _dot(_pack3(sg, lane_lt32), e_beta[...])
        gc128 = _dot(pk_ac, e_g[...])

        ps = _dot(hn_scr[slot, rows, :], w_a[:, W_POOL:W_POOL + W_SSD])
        u_ssd = jnp.concatenate([ps[:, :D_MODEL], ps[:, 2 * D_MODEL:]], axis=1)
        z_ssd = ps[:, D_MODEL:2 * D_MODEL]
        gate_mid = 3 * D_MODEL // 2
        gate_a = _dot(_after(hn_scr[slot, rows, :], ps[0:8, 0:128]), w_gate[:, 0:gate_mid])
        acts = []
        for s, b0, x0, ln in seqs:
            sbuf[s, b0 + HEAD_ROWS:b0 + HEAD_ROWS + ln, :] = u_ssd[x0:x0 + ln]
            o_sconv[s] = u_ssd[x0 + ln - CONV_HIST:x0 + ln]
            acts.append(_silu(_causal_conv(sbuf, s, b0 + HEAD_ROWS, ln, sconvw[...], 4) + sconvb[...]))
        xbc = jnp.concatenate(acts, axis=0) if len(seqs) > 1 else acts[0]
        xc, bm, cm_ = xbc[:, :D_MODEL], xbc[:, D_MODEL:D_MODEL + 256], xbc[:, D_MODEL + 256:]
        br_pool = _dot(_after(y_pool, xc[0:8, 0:128]), wbr[0])

        cr = [slice(c * CHUNK, (c + 1) * CHUNK) for c in range(n_chunks)]
        acs = [acum_exp[r] for r in cr]
        a_rows = [jnp.sum(ac * diag_sel, axis=0, keepdims=True) for ac in acs]
        l_alls = [jnp.where(causal, jnp.exp(ac - a_row), 0.0) for ac, a_row in zip(acs, a_rows)]
        bcs = [bm[r].astype(BF16) for r in cr]
        ccs = [cm_[r].astype(BF16) for r in cr]
        cbts = [jnp.concatenate(
            [_dot_nt(cc[:, 128 * g:128 * (g + 1)], jnp.concatenate([bc[:, 128 * g:128 * (g + 1)]] * 8, axis=0))
             for g in range(SSD_G)], axis=1) for bc, cc in zip(bcs, ccs)]
        m_alls = [(cbt * l_all).astype(BF16) for cbt, l_all in zip(cbts, l_alls)]
        xdts = [xc[r] * dt_exp[r] for r in cr]
        zb = jnp.zeros((CHUNK, 128), BF16)
        y_intras = []
        for m_all, xdt in zip(m_alls, xdts):
            xdtb = xdt.astype(BF16)
            y_parts = []
            for j in range(8):
                xp = xdtb[:, 128 * j:128 * (j + 1)]
                rhs = jnp.concatenate([jnp.where(lo_half64, xp, zb), jnp.where(lo_half64, zb, xp)], axis=0)
                y_parts.append(_dot(m_all[:, 128 * j:128 * (j + 1)], rhs))
            y_intras.append(jnp.concatenate(y_parts, axis=1))
        a_lasts = [ac[CHUNK - 1:CHUNK, :] for ac in acs]
        xds = [(xdt * jnp.exp(a_last - ac)).astype(BF16) for xdt, a_last, ac in zip(xdts, a_lasts, acs)]
        upds = [jnp.concatenate(
            [_dot_tn(bc[:, 128 * g:128 * (g + 1)], xd[:, 512 * g:512 * (g + 1)]) for g in range(SSD_G)], axis=1)
            for bc, xd in zip(bcs, xds)]

        st = None
        for c in range(n_chunks):
            r = cr[c]
            s = c // chunks_per_seq
            if c % chunks_per_seq == 0:
                st = st_scr[...] if prompt else ssd_s[s].T
            stb = st.astype(BF16)
            y = y_intras[c] + jnp.exp(acs[c]) * jnp.concatenate(
                [_dot(ccs[c][:, 128 * g:128 * (g + 1)], stb[:, 512 * g:512 * (g + 1)])
                 for g in range(SSD_G)], axis=1)
            st = jnp.exp(a_lasts[c]) * st + upds[c]
            if (c + 1) % chunks_per_seq == 0:
                if prompt:
                    st_scr[...] = st
                else:
                    o_ssd[s] = st.T
            y = y + ssd_d[...] * xc[r]
            yz = y * _silu(z_ssd[r])
            yn = jnp.concatenate(
                [yz[:, 512 * g:512 * (g + 1)]
                 * lax.rsqrt(jnp.mean(yz[:, 512 * g:512 * (g + 1)] ** 2, axis=-1, keepdims=True) + RMS_EPS)
                 for g in range(SSD_G)], axis=1)
            yssd[r0 + c * CHUNK:r0 + (c + 1) * CHUNK, :] = yn * ssdnw[...]
        y_ssd = yssd[rows, :].astype(BF16)

        pg = _dot(hn_scr[slot, rows, :], w_gdn[...])
        u_gdn, z_gdn = pg[:, :GDN_CONV], pg[:, GDN_CONV:]
        gate_b = _dot(_after(hn_scr[slot, rows, :], pg[0:8, 0:128]), w_gate[:, gate_mid:])
        acts = []
        for s, b0, x0, ln in seqs:
            gbuf[s, b0 + HEAD_ROWS:b0 + HEAD_ROWS + ln, :] = u_gdn[x0:x0 + ln]
            o_gconv[s] = u_gdn[x0 + ln - CONV_HIST:x0 + ln]
            acts.append(_silu(_causal_conv(gbuf, s, b0 + HEAD_ROWS, ln, gconvw[...], 4)))
        qkv = jnp.concatenate(acts, axis=0) if len(seqs) > 1 else acts[0]

        qn = [l2n(qkv[:, 128 * j:128 * (j + 1)]) * (GDN_DK ** -0.5) for j in range(4)]
        kn = [l2n(qkv[:, 512 + 128 * j:512 + 128 * (j + 1)]) for j in range(4)]
        vc = qkv[:, 1024:]

        combos = [(c, j) for c in range(n_chunks) for j in range(4)]
        pre = {}
        lps = []
        for c, j in combos:
            r = slice(c * CHUNK, (c + 1) * CHUNK)
            h0, h1 = 2 * j, 2 * j + 1
            kj, qj = kn[j][r], qn[j][r]
            g0 = gc128[r, 128 * h0:128 * (h0 + 1)]
            g1 = gc128[r, 128 * h1:128 * (h1 + 1)]
            kst = jnp.concatenate([kj, kj], axis=0)
            qst = jnp.concatenate([qj, qj], axis=0)
            bst = jnp.concatenate([beta128[r, 128 * h0:128 * (h0 + 1)],
                                   beta128[r, 128 * h1:128 * (h1 + 1)]], axis=0)
            gst = jnp.concatenate([g0, g1], axis=0)
            vst = jnp.concatenate([vc[r, 128 * h0:128 * (h0 + 1)], vc[r, 128 * h1:128 * (h1 + 1)]], axis=0)
            g_row = jnp.sum(jnp.where(lo_half64, g0, g1) * diag_sel2, axis=0, keepdims=True)
            dec = jnp.exp(gst - g_row)
            eg = jnp.exp(gst)
            kq = _dot_nt(jnp.concatenate([kj, qj], axis=0).astype(BF16), kst.astype(BF16))
            kk2 = jnp.concatenate([kq[:CHUNK], kq[:CHUNK]], axis=0)
            qk2 = jnp.concatenate([kq[CHUNK:], kq[CHUNK:]], axis=0)
            lps.append(jnp.where(strict, kk2 * bst * dec, 0.0))
            gl0, gl1 = g0[CHUNK - 1:CHUNK, :], g1[CHUNK - 1:CHUNK, :]
            pre[(c, j)] = dict(
                ap=jnp.where(incl, qk2 * dec, 0.0).astype(BF16),
                rhs=jnp.concatenate([vst * bst, kst * bst * eg], axis=1).astype(BF16),
                qg=(qst * eg).astype(BF16),
                kjb=kj.astype(BF16),
                e_st=jnp.exp(jnp.concatenate([gl0 - g0, gl1 - g1], axis=0)),
                dl0=jnp.exp(gl0), dl1=jnp.exp(gl1))
        t_invs = _neumann_inverse([-lp for lp in lps], eye)
        for (c, j), t_inv in zip(combos, t_invs):
            uw = _dot(t_inv.astype(BF16), pre[(c, j)]["rhs"])
            pre[(c, j)]["u"] = uw[:, :128]
            pre[(c, j)]["w"] = uw[:, 128:].astype(BF16)

        gs = [None] * GDN_HV
        for c in range(n_chunks):
            s = c // chunks_per_seq
            if c % chunks_per_seq == 0:
                for h in range(GDN_HV):
                    gs[h] = o_gdn[0, h] if prompt else gdn_s[s, h]
            rr = []
            for j in range(4):
                p_ = pre[(c, j)]
                for hh in range(2):
                    half = slice(hh * CHUNK, (hh + 1) * CHUNK)
                    rr.append(_dot(jnp.concatenate([p_["w"][half], p_["qg"][half]], axis=0),
                                   gs[2 * j + hh].astype(BF16)))
            vnews = [pre[(c, j)]["u"] - jnp.concatenate([rr[2 * j][:CHUNK], rr[2 * j + 1][:CHUNK]], axis=0)
                     for j in range(4)]
            upds = []
            for j in range(4):
                vn_e = vnews[j] * pre[(c, j)]["e_st"]
                upds.append(_dot_tn(pre[(c, j)]["kjb"],
                                    jnp.concatenate([vn_e[:CHUNK], vn_e[CHUNK:]], axis=1).astype(BF16)))
            for j in range(4):
                h0, h1 = 2 * j, 2 * j + 1
                gs[h0] = pre[(c, j)]["dl0"] * gs[h0] + upds[j][:, :128]
                gs[h1] = pre[(c, j)]["dl1"] * gs[h1] + upds[j][:, 128:]
            yr = slice(r0 + c * CHUNK, r0 + (c + 1) * CHUNK)
            for j in range(4):
                h0, h1 = 2 * j, 2 * j + 1
                o_st = (jnp.concatenate([rr[2 * j][CHUNK:], rr[2 * j + 1][CHUNK:]], axis=0)
                        + _dot(pre[(c, j)]["ap"], vnews[j].astype(BF16)))
                ygdn[yr, 128 * h0:128 * (h0 + 1)] = o_st[:CHUNK]
                ygdn[yr, 128 * h1:128 * (h1 + 1)] = o_st[CHUNK:]
            if (c + 1) % chunks_per_seq == 0:
                for h in range(GDN_HV):
                    o_gdn[s, h] = gs[h]
        og = ygdn[rows, :]
        og = jnp.concatenate(
            [og[:, 128 * h:128 * (h + 1)]
             * lax.rsqrt(jnp.mean(og[:, 128 * h:128 * (h + 1)] ** 2, axis=-1, keepdims=True) + RMS_EPS)
             for h in range(GDN_HV)], axis=1)
        y_gdn = (og * gdnnw[...] * _silu(z_gdn)).astype(BF16)

        gates = jax.nn.sigmoid(jnp.concatenate([gate_a, gate_b], axis=1))
        merged = (gates[:, :D_MODEL] * br_pool
                  + gates[:, D_MODEL:2 * D_MODEL] * _dot(y_ssd, wbr[1])
                  + gates[:, 2 * D_MODEL:] * _dot(y_gdn, wbr[2]))
        xn = x + _dot(merged.astype(BF16), wout[...])
        if final_norm:
            xn = xn * lax.rsqrt(jnp.mean(xn * xn, axis=-1, keepdims=True) + RMS_EPS) * fnw[...]
        xo[rows, :] = xn

    for sub in range(n_sub):
        sub_block(sub)

    hn_scr[1 - slot] = normed(xnext_ref[...])

    if prompt:
        abuf[0, 0:HEAD_ROWS, :] = abuf[0, tb:tb + HEAD_ROWS, :]
        sbuf[0, 0:HEAD_ROWS, :] = sbuf[0, tb:tb + HEAD_ROWS, :]
        gbuf[0, 0:HEAD_ROWS, :] = gbuf[0, tb:tb + HEAD_ROWS, :]

        @pl.when(step == pl.num_programs(0) - 1)
        def _final_state():
            o_ssd[0] = st_scr[...].T


def _layer_spec(arr, layer):
    nd = arr.ndim - 1
    return pl.BlockSpec((None,) + arr.shape[1:], lambda i, _l=layer, _nd=nd: (_l,) + (0,) * _nd,
                        pipeline_mode=pl.Buffered(1))


def _const_spec(arr):
    nd = arr.ndim
    return pl.BlockSpec(arr.shape, lambda i, _nd=nd: (0,) * _nd, pipeline_mode=pl.Buffered(1))


def _run_layer(layer, x2d, states, weights, fnw, consts, stacked, *, prompt, final_norm):
    tokens = x2d.shape[0]
    if prompt:
        n_seq, seq_len, n_state, n_sub = 1, PROMPT_BLOCK, 1, PROMPT_SUB_BLOCKS
    else:
        n_seq, seq_len, n_state, n_sub = SAMPLE_SEQS, CHUNK, tokens // CHUNK, 1
    tb = n_seq * seq_len
    grid = (tokens // tb,)

    def state_spec(shape):
        nd = len(shape)
        if prompt:
            return pl.BlockSpec((None, 1) + shape, lambda i, _nd=nd: (layer, 0) + (0,) * _nd)
        return pl.BlockSpec((None, n_seq) + shape, lambda i, _nd=nd: (layer, i) + (0,) * _nd)

    last_block = tokens // tb - 1
    in_specs = [pl.BlockSpec((tb, D_MODEL), lambda i: (i, 0)),
                pl.BlockSpec((tb, D_MODEL), lambda i: (jnp.minimum(i + 1, last_block), 0))]
    args = [x2d, x2d]
    if not prompt:
        in_specs += [state_spec(s) for s in STATE_SHAPES]
        args += list(states)
    in_specs += [_layer_spec(w, layer) for w in weights] + [_const_spec(fnw)] + [_const_spec(c) for c in consts]
    args += list(weights) + [fnw] + list(consts)
    aliases = {}
    for k, buf in enumerate(stacked):
        aliases[len(args)] = 1 + k
        in_specs.append(pl.BlockSpec(memory_space=pl.ANY))
        args.append(buf)

    out_shape = [jax.ShapeDtypeStruct((tokens, D_MODEL), F32)]
    out_shape += [jax.ShapeDtypeStruct((DEPTH, n_state) + s, F32) for s in STATE_SHAPES]
    out_specs = [pl.BlockSpec((tb, D_MODEL), lambda i: (i, 0))] + [state_spec(s) for s in STATE_SHAPES]

    scratch = [pltpu.VMEM((n_seq, HEAD_ROWS + seq_len, D_MODEL), F32),
               pltpu.VMEM((n_seq, HEAD_ROWS + seq_len, SSD_CONV), F32),
               pltpu.VMEM((n_seq, HEAD_ROWS + seq_len, GDN_CONV), F32),
               pltpu.VMEM((tb, D_MODEL), F32),
               pltpu.VMEM((tb, D_MODEL), F32),
               pltpu.VMEM((SSD_N, D_MODEL), F32),
               pltpu.VMEM((2, tb, D_MODEL), BF16)]
    kern = functools.partial(_layer_kernel, prompt=prompt, n_seq=n_seq, seq_len=seq_len, n_sub=n_sub,
                             final_norm=final_norm, n_alias=len(aliases))
    return pl.pallas_call(
        kern, grid=grid, in_specs=in_specs, out_specs=out_specs, out_shape=out_shape,
        scratch_shapes=scratch, input_output_aliases=aliases,
        compiler_params=pltpu.CompilerParams(dimension_semantics=("arbitrary",),
                                             vmem_limit_bytes=VMEM_LIMIT_BYTES),
    )(*args)


def _expand_matrix(first_lane, n_heads, width):
    rows = jnp.arange(SMALL_W)
    src = rows % 32
    head = src - first_lane
    valid = (rows < 96) & (head >= 0) & (head < n_heads)
    cols = jnp.arange(n_heads * width) // width
    return (valid[:, None] & (head[:, None] == cols[None, :])).astype(BF16)


def _block_tri(tb):
    i = jnp.arange(tb)
    return ((i[:, None] // CHUNK == i[None, :] // CHUNK) & (i[None, :] <= i[:, None])).astype(BF16)


def _all_layer_weights(norm_w, w_in, pool_w, pool_scale, ssd_conv_w, ssd_conv_b, ssd_dt_bias, ssd_A_log,
                       ssd_D, ssd_norm_w, gdn_conv_w, gdn_dt_bias, gdn_A_log, gdn_norm_w, w_br_pool,
                       w_br_ssd, w_br_gdn, w_out):
    n_pad = SMALL_W - SSD_H - 2 * GDN_HV
    w_a = w_in[:, :, 0:IN_DT].astype(BF16)
    w_gdn = w_in[:, :, IN_GDN:IN_BETA].astype(BF16)
    w_gate = w_in[:, :, IN_GATE:IN_END].astype(BF16)
    w_small = jnp.concatenate([w_in[:, :, IN_DT:IN_GDN], w_in[:, :, IN_BETA:IN_GATE],
                               jnp.zeros((DEPTH, D_MODEL, n_pad), F32)], axis=2).astype(BF16)
    pad8 = jnp.zeros((DEPTH, GDN_HV), F32)
    pad = jnp.zeros((DEPTH, n_pad), F32)
    sbias = jnp.concatenate([ssd_dt_bias, pad8, gdn_dt_bias, pad], axis=1)[:, None, :]
    salog = jnp.concatenate([ssd_A_log, pad8, gdn_A_log, pad], axis=1)[:, None, :]
    return [
        norm_w[:, None, :], w_a, w_small, w_gdn, w_gate, pool_w.astype(BF16), pool_scale[:, None, :],
        ssd_conv_w, ssd_conv_b[:, None, :], sbias, salog,
        jnp.repeat(ssd_D, SSD_P, axis=1)[:, None, :], ssd_norm_w[:, None, :],
        gdn_conv_w, jnp.tile(gdn_norm_w, (1, GDN_HV))[:, None, :],
        jnp.stack([w_br_pool, w_br_ssd, w_br_gdn], axis=1).astype(BF16), w_out.astype(BF16),
    ]


def kernel(x_prompt, x_sample, state_pool, state_ssd_conv, state_ssd, state_gdn_conv, state_gdn, norm_w, w_in,
           pool_w, pool_scale, ssd_conv_w, ssd_conv_b, ssd_dt_bias, ssd_A_log, ssd_D, ssd_norm_w, gdn_conv_w,
           gdn_dt_bias, gdn_A_log, gdn_norm_w, w_br_pool, w_br_ssd, w_br_gdn, w_out, final_norm_w):
    bp, seq, _ = x_prompt.shape
    bs, dseq, _ = x_sample.shape
    assert bp == 1 and dseq == CHUNK and seq % PROMPT_BLOCK == 0 and bs % SAMPLE_SEQS == 0
    e_mats = [_expand_matrix(0, SSD_H, SSD_P), _expand_matrix(SSD_H, GDN_HV, GDN_DK),
              _expand_matrix(SSD_H + GDN_HV, GDN_HV, GDN_DK)]
    consts_p = [_block_tri(PROMPT_BLOCK // PROMPT_SUB_BLOCKS)] + e_mats
    consts_s = [_block_tri(SAMPLE_SEQS * CHUNK)] + e_mats
    weights = _all_layer_weights(norm_w, w_in, pool_w, pool_scale, ssd_conv_w, ssd_conv_b, ssd_dt_bias,
                                 ssd_A_log, ssd_D, ssd_norm_w, gdn_conv_w, gdn_dt_bias, gdn_A_log, gdn_norm_w,
                                 w_br_pool, w_br_ssd, w_br_gdn, w_out)
    fnw = final_norm_w[None, :]
    states_s = (state_pool, state_ssd_conv, state_ssd.reshape(DEPTH, bs, D_MODEL, SSD_N),
                state_gdn_conv, state_gdn)

    hp = x_prompt.reshape(seq, D_MODEL)
    hs = x_sample.reshape(bs * dseq, D_MODEL)
    stk_p = [jnp.zeros((DEPTH, 1) + s, F32) for s in STATE_SHAPES]
    stk_s = [jnp.zeros((DEPTH, bs) + s, F32) for s in STATE_SHAPES]
    for l in range(DEPTH):
        last = l == DEPTH - 1
        rp = _run_layer(l, hp, None, weights, fnw, consts_p, stk_p, prompt=True, final_norm=last)
        rs = _run_layer(l, hs, states_s, weights, fnw, consts_s, stk_s, prompt=False, final_norm=last)
        hp, stk_p = rp[0], rp[1:]
        hs, stk_s = rs[0], rs[1:]

    def ssd_out(st):
        return st.reshape(DEPTH, st.shape[1], SSD_H, SSD_P, SSD_N)

    return (hp.reshape(bp, seq, D_MODEL), hs.reshape(bs, dseq, D_MODEL),
            stk_p[0], stk_s[0], stk_p[1], stk_s[1], ssd_out(stk_p[2]), ssd_out(stk_s[2]),
            stk_p[3], stk_s[3], stk_p[4], stk_s[4])
```

```python
import functools

import jax
import jax.numpy as jnp
from jax import lax
from jax.experimental import pallas as pl
from jax.experimental.pallas import tpu as pltpu

F32 = jnp.float32
BF16 = jnp.bfloat16

D_MODEL = 1024
DEPTH = 4
CHUNK = 64
RMS_EPS = 1e-6
POOL_HIST = 15
CONV_HIST = 3
HEAD_ROWS = 16
SSD_H, SSD_P, SSD_N, SSD_G = 16, 64, 128, 2
SSD_CONV = 1536
GDN_HV, GDN_DK, GDN_DV = 8, 128, 128
GDN_CONV = 2048
SMALL_W = 128

W_POOL, W_SSD = 2048, 2560
IN_DT, IN_GDN, IN_BETA, IN_GATE, IN_END = 4608, 4624, 7696, 7712, 10784

PROMPT_BLOCK = 256
PROMPT_SUB_BLOCKS = 1
SAMPLE_SEQS = 2
VMEM_LIMIT_BYTES = 60 * 1024 * 1024

STATE_SHAPES = ((POOL_HIST, D_MODEL), (CONV_HIST, SSD_CONV), (D_MODEL, SSD_N),
                (CONV_HIST, GDN_CONV), (GDN_HV, GDN_DK, GDN_DV))


def _dot(a, b):
    return jnp.dot(a, b, preferred_element_type=F32)


def _dot_nt(a, b):
    return lax.dot_general(a, b, (((1,), (1,)), ((), ())), preferred_element_type=F32)


def _dot_tn(a, b):
    return lax.dot_general(a, b, (((0,), (0,)), ((), ())), preferred_element_type=F32)


def _silu(x):
    return x * jax.nn.sigmoid(x)


def _softplus(x):
    return jnp.maximum(x, 0.0) + jnp.log1p(jnp.exp(-jnp.abs(x)))


def _split3(a):
    hi = a.astype(BF16)
    r = a - hi.astype(F32)
    mid = r.astype(BF16)
    lo = (r - mid.astype(F32)).astype(BF16)
    return hi, mid, lo


def _pack3(a, lane_lt32):
    hi, mid, lo = _split3(a)
    z = jnp.zeros_like(a)
    p = (jnp.where(lane_lt32, hi.astype(F32), z)
         + pltpu.roll(jnp.where(lane_lt32, mid.astype(F32), z), 32, 1)
         + pltpu.roll(jnp.where(lane_lt32, lo.astype(F32), z), 64, 1))
    return p.astype(BF16)


def _causal_conv(buf, s, row, length, w, k_taps):
    acc = buf[s, row:row + length, :] * w[k_taps - 1:k_taps, :]
    for j in range(1, k_taps):
        acc = acc + buf[s, row - j:row - j + length, :] * w[k_taps - 1 - j:k_taps - j, :]
    return acc


def _after(a_bf16, dep):
    u = pltpu.bitcast(dep, jnp.uint32)
    zero = lax.shift_right_logical(lax.shift_right_logical(u, jnp.uint32(16)), jnp.uint32(16))
    bits = pltpu.bitcast(a_bf16, jnp.uint32)
    reps = (bits.shape[0] // zero.shape[0], bits.shape[1] // zero.shape[1])
    return pltpu.bitcast(bits | jnp.tile(zero, reps), BF16)


def _neumann_inverse(xs, eye):
    ps = [eye + x for x in xs]
    xbs = [x.astype(BF16) for x in xs]
    ys = [_dot(xb, xb) for xb in xbs]
    for _ in range(4):
        ybs = [y.astype(BF16) for y in ys]
        rs = [_dot(jnp.concatenate([yb, p.astype(BF16)], axis=0), yb) for yb, p in zip(ybs, ps)]
        ys = [r[:128] for r in rs]
        ps = [p + r[128:] for p, r in zip(ps, rs)]
    return [p + _dot(p.astype(BF16), y.astype(BF16)) for p, y in zip(ps, ys)]


def _layer_kernel(*refs, prompt, n_seq, seq_len, n_sub, final_norm, n_alias):
    tb = n_seq * seq_len
    tbs = tb // n_sub
    n_chunks = tbs // CHUNK
    it = iter(refs)
    x_ref = next(it)
    if not prompt:
        pool_h, sconv_h, ssd_s, gconv_h, gdn_s = (next(it) for _ in range(5))
    (normw, w_a, w_small, w_gdn, w_gate, poolw, poolsc, sconvw, sconvb, sbias, salog, ssd_d, ssdnw, gconvw,
     gdnnw, wbr, wout, fnw, btri, e_dt, e_beta, e_g) = (next(it) for _ in range(22))
    for _ in range(n_alias):
        next(it)
    xo, o_pool, o_sconv, o_ssd, o_gconv, o_gdn = (next(it) for _ in range(6))
    abuf, sbuf, gbuf, yssd, ygdn, st_scr, hn_scr = (next(it) for _ in range(7))

    step = pl.program_id(0)
    if prompt:
        @pl.when(step == 0)
        def _init():
            abuf[:, 0:HEAD_ROWS, :] = jnp.zeros((n_seq, HEAD_ROWS, D_MODEL), F32)
            sbuf[:, 0:HEAD_ROWS, :] = jnp.zeros((n_seq, HEAD_ROWS, SSD_CONV), F32)
            gbuf[:, 0:HEAD_ROWS, :] = jnp.zeros((n_seq, HEAD_ROWS, GDN_CONV), F32)
            st_scr[...] = jnp.zeros(st_scr.shape, F32)
            o_gdn[...] = jnp.zeros(o_gdn.shape, F32)
    else:
        abuf[:, 0:HEAD_ROWS, :] = jnp.zeros((n_seq, HEAD_ROWS, D_MODEL), F32)
        sbuf[:, 0:HEAD_ROWS, :] = jnp.zeros((n_seq, HEAD_ROWS, SSD_CONV), F32)
        gbuf[:, 0:HEAD_ROWS, :] = jnp.zeros((n_seq, HEAD_ROWS, GDN_CONV), F32)
        abuf[:, HEAD_ROWS - POOL_HIST:HEAD_ROWS, :] = pool_h[...]
        sbuf[:, HEAD_ROWS - CONV_HIST:HEAD_ROWS, :] = sconv_h[...]
        gbuf[:, HEAD_ROWS - CONV_HIST:HEAD_ROWS, :] = gconv_h[...]

    row16 = lax.broadcasted_iota(jnp.int32, (HEAD_ROWS, 256), 0).astype(F32)
    lane = lax.broadcasted_iota(jnp.int32, (1, SMALL_W), 1)
    is_decay = (lane < SSD_H) | ((lane >= SSD_H + GDN_HV) & (lane < SSD_H + 2 * GDN_HV))
    lane_lt32 = lane < 32
    q_idx = lax.broadcasted_iota(jnp.int32, (CHUNK, D_MODEL), 0)
    s_idx = lax.broadcasted_iota(jnp.int32, (CHUNK, D_MODEL), 1) & (CHUNK - 1)
    causal = s_idx <= q_idx
    diag_sel = (s_idx == q_idx).astype(F32)
    ri = lax.broadcasted_iota(jnp.int32, (128, 128), 0)
    ci = lax.broadcasted_iota(jnp.int32, (128, 128), 1)
    same = (ri >> 6) == (ci >> 6)
    incl = same & ((ci & 63) <= (ri & 63))
    strict = same & ((ci & 63) < (ri & 63))
    eye = (ri == ci).astype(F32)
    q64 = lax.broadcasted_iota(jnp.int32, (CHUNK, 128), 0)
    l64 = lax.broadcasted_iota(jnp.int32, (CHUNK, 128), 1)
    diag_sel2 = ((l64 & 63) == q64).astype(F32)
    lo_half64 = l64 < CHUNK

    def l2n(t):
        return t * lax.rsqrt(jnp.sum(t * t, axis=-1, keepdims=True) + 1e-6)

    def sub_block(sub):
        r0 = sub * tbs
        rows = slice(r0, r0 + tbs)
        if prompt:
            seqs = [(0, r0, 0, tbs)]
        else:
            seqs = [(s, 0, s * seq_len, seq_len) for s in range(n_seq)]
        chunks_per_seq = seqs[0][3] // CHUNK

        x = x_ref[rows, :]
        hn_scr[rows, :] = (x * lax.rsqrt(jnp.mean(x * x, axis=-1, keepdims=True) + RMS_EPS)
                           * normw[...]).astype(BF16)
        pp = _dot(hn_scr[rows, :], w_a[:, 0:W_POOL])
        sm = _dot(hn_scr[rows, :], w_small[...])

        a_in, z_pool = pp[:, :D_MODEL], pp[:, D_MODEL:]
        y_pool_parts = []
        for s, b0, x0, ln in seqs:
            abuf[s, b0 + HEAD_ROWS:b0 + HEAD_ROWS + ln, :] = a_in[x0:x0 + ln]
            full = abuf[s, b0:b0 + HEAD_ROWS + ln, :]
            o_pool[s] = full[ln + HEAD_ROWS - POOL_HIST:, :]
            ys = []
            for g in range(4):
                win = 2 << g
                fg = full[:, 256 * g:256 * (g + 1)]
                acc = fg
                for k in range(g + 1):
                    acc = acc + pltpu.roll(acc, 1 << k, 0)
                wsum = acc[HEAD_ROWS:]
                cur = fg[HEAD_ROWS:]
                inv_w = 1.0 / win
                if prompt and sub == 0:
                    fac = jnp.where(step == 0, 1.0 / jnp.minimum(row16 + 1.0, float(win)), inv_w)
                    d = jnp.concatenate([wsum[:HEAD_ROWS] * fac - cur[:HEAD_ROWS],
                                         wsum[HEAD_ROWS:] * inv_w - cur[HEAD_ROWS:]], axis=0)
                else:
                    d = wsum * inv_w - cur
                ys.append(_dot(d.astype(BF16), poolw[g]))
            y_pool_parts.append(jnp.concatenate(ys, axis=1))
        y_pool = jnp.concatenate(y_pool_parts, axis=0) if len(seqs) > 1 else y_pool_parts[0]
        y_pool = (y_pool * poolsc[...] * _silu(z_pool)).astype(BF16)

        sp = _softplus(sm + sbias[...])
        sg = jax.nn.sigmoid(sm)
        av = sp * jnp.where(is_decay, -jnp.exp(salog[...]), 0.0)
        a_hi, a_mid, a_lo = _split3(av)
        bt = btri[...]
        acum = _dot(bt, a_hi) + _dot(bt, a_mid) + _dot(bt, a_lo)
        pk_ac = _pack3(acum, lane_lt32)
        dt_exp = _dot(_pack3(sp, lane_lt32), e_dt[...])
        acum_exp = _dot(pk_ac, e_dt[...])
        beta128 = _dot(_pack3(sg, lane_lt32), e_beta[...])
        gc128 = _dot(pk_ac, e_g[...])

        ps = _dot(hn_scr[rows, :], w_a[:, W_POOL:W_POOL + W_SSD])
        u_ssd = jnp.concatenate([ps[:, :D_MODEL], ps[:, 2 * D_MODEL:]], axis=1)
        z_ssd = ps[:, D_MODEL:2 * D_MODEL]
        gate_mid = 3 * D_MODEL // 2
        gate_a = _dot(_after(hn_scr[rows, :], ps[0:8, 0:128]), w_gate[:, 0:gate_mid])
        acts = []
        for s, b0, x0, ln in seqs:
            sbuf[s, b0 + HEAD_ROWS:b0 + HEAD_ROWS + ln, :] = u_ssd[x0:x0 + ln]
            o_sconv[s] = u_ssd[x0 + ln - CONV_HIST:x0 + ln]
            acts.append(_silu(_causal_conv(sbuf, s, b0 + HEAD_ROWS, ln, sconvw[...], 4) + sconvb[...]))
        xbc = jnp.concatenate(acts, axis=0) if len(seqs) > 1 else acts[0]
        xc, bm, cm_ = xbc[:, :D_MODEL], xbc[:, D_MODEL:D_MODEL + 256], xbc[:, D_MODEL + 256:]
        br_pool = _dot(_after(y_pool, xc[0:8, 0:128]), wbr[0])

        cr = [slice(c * CHUNK, (c + 1) * CHUNK) for c in range(n_chunks)]
        acs = [acum_exp[r] for r in cr]
        a_rows = [jnp.sum(ac * diag_sel, axis=0, keepdims=True) for ac in acs]
        l_alls = [jnp.where(causal, jnp.exp(ac - a_row), 0.0) for ac, a_row in zip(acs, a_rows)]
        bcs = [bm[r].astype(BF16) for r in cr]
        ccs = [cm_[r].astype(BF16) for r in cr]
        cbts = [jnp.concatenate(
            [_dot_nt(cc[:, 128 * g:128 * (g + 1)], jnp.concatenate([bc[:, 128 * g:128 * (g + 1)]] * 8, axis=0))
             for g in range(SSD_G)], axis=1) for bc, cc in zip(bcs, ccs)]
        m_alls = [(cbt * l_all).astype(BF16) for cbt, l_all in zip(cbts, l_alls)]
        xdts = [xc[r] * dt_exp[r] for r in cr]
        zb = jnp.zeros((CHUNK, 128), BF16)
        y_intras = []
        for m_all, xdt in zip(m_alls, xdts):
            xdtb = xdt.astype(BF16)
            y_parts = []
            for j in range(8):
                xp = xdtb[:, 128 * j:128 * (j + 1)]
                rhs = jnp.concatenate([jnp.where(lo_half64, xp, zb), jnp.where(lo_half64, zb, xp)], axis=0)
                y_parts.append(_dot(m_all[:, 128 * j:128 * (j + 1)], rhs))
            y_intras.append(jnp.concatenate(y_parts, axis=1))
        a_lasts = [ac[CHUNK - 1:CHUNK, :] for ac in acs]
        xds = [(xdt * jnp.exp(a_last - ac)).astype(BF16) for xdt, a_last, ac in zip(xdts, a_lasts, acs)]
        upds = [jnp.concatenate(
            [_dot_tn(bc[:, 128 * g:128 * (g + 1)], xd[:, 512 * g:512 * (g + 1)]) for g in range(SSD_G)], axis=1)
            for bc, xd in zip(bcs, xds)]

        st = None
        for c in range(n_chunks):
            r = cr[c]
            s = c // chunks_per_seq
            if c % chunks_per_seq == 0:
                st = st_scr[...] if prompt else ssd_s[s].T
            stb = st.astype(BF16)
            y = y_intras[c] + jnp.exp(acs[c]) * jnp.concatenate(
                [_dot(ccs[c][:, 128 * g:128 * (g + 1)], stb[:, 512 * g:512 * (g + 1)])
                 for g in range(SSD_G)], axis=1)
            st = jnp.exp(a_lasts[c]) * st + upds[c]
            if (c + 1) % chunks_per_seq == 0:
                if prompt:
                    st_scr[...] = st
                else:
                    o_ssd[s] = st.T
            y = y + ssd_d[...] * xc[r]
            yz = y * _silu(z_ssd[r])
            yn = jnp.concatenate(
                [yz[:, 512 * g:512 * (g + 1)]
                 * lax.rsqrt(jnp.mean(yz[:, 512 * g:512 * (g + 1)] ** 2, axis=-1, keepdims=True) + RMS_EPS)
                 for g in range(SSD_G)], axis=1)
            yssd[r0 + c * CHUNK:r0 + (c + 1) * CHUNK, :] = yn * ssdnw[...]
        y_ssd = yssd[rows, :].astype(BF16)

        pg = _dot(hn_scr[rows, :], w_gdn[...])
        u_gdn, z_gdn = pg[:, :GDN_CONV], pg[:, GDN_CONV:]
        gate_b = _dot(_after(hn_scr[rows, :], pg[0:8, 0:128]), w_gate[:, gate_mid:])
        acts = []
        for s, b0, x0, ln in seqs:
            gbuf[s, b0 + HEAD_ROWS:b0 + HEAD_ROWS + ln, :] = u_gdn[x0:x0 + ln]
            o_gconv[s] = u_gdn[x0 + ln - CONV_HIST:x0 + ln]
            acts.append(_silu(_causal_conv(gbuf, s, b0 + HEAD_ROWS, ln, gconvw[...], 4)))
        qkv = jnp.concatenate(acts, axis=0) if len(seqs) > 1 else acts[0]

        qn = [l2n(qkv[:, 128 * j:128 * (j + 1)]) * (GDN_DK ** -0.5) for j in range(4)]
        kn = [l2n(qkv[:, 512 + 128 * j:512 + 128 * (j + 1)]) for j in range(4)]
        vc = qkv[:, 1024:]

        combos = [(c, j) for c in range(n_chunks) for j in range(4)]
        pre = {}
        kqs = [_dot_nt(jnp.concatenate([kn[j][c * CHUNK:(c + 1) * CHUNK], qn[j][c * CHUNK:(c + 1) * CHUNK]],
                                       axis=0).astype(BF16),
                       jnp.concatenate([kn[j][c * CHUNK:(c + 1) * CHUNK]] * 2, axis=0).astype(BF16))
               for c, j in combos]
        lps = []
        for (c, j), kq in zip(combos, kqs):
            r = slice(c * CHUNK, (c + 1) * CHUNK)
            h0, h1 = 2 * j, 2 * j + 1
            kj, qj = kn[j][r], qn[j][r]
            g0 = gc128[r, 128 * h0:128 * (h0 + 1)]
            g1 = gc128[r, 128 * h1:128 * (h1 + 1)]
            kst = jnp.concatenate([kj, kj], axis=0)
            qst = jnp.concatenate([qj, qj], axis=0)
            bst = jnp.concatenate([beta128[r, 128 * h0:128 * (h0 + 1)],
                                   beta128[r, 128 * h1:128 * (h1 + 1)]], axis=0)
            gst = jnp.concatenate([g0, g1], axis=0)
            vst = jnp.concatenate([vc[r, 128 * h0:128 * (h0 + 1)], vc[r, 128 * h1:128 * (h1 + 1)]], axis=0)
            g_row = jnp.sum(jnp.where(lo_half64, g0, g1) * diag_sel2, axis=0, keepdims=True)
            dec = jnp.exp(gst - g_row)
            eg = jnp.exp(gst)
            kk2 = jnp.concatenate([kq[:CHUNK], kq[:CHUNK]], axis=0)
            qk2 = jnp.concatenate([kq[CHUNK:], kq[CHUNK:]], axis=0)
            lps.append(jnp.where(strict, kk2 * bst * dec, 0.0))
            gl0, gl1 = g0[CHUNK - 1:CHUNK, :], g1[CHUNK - 1:CHUNK, :]
            pre[(c, j)] = dict(
                ap=jnp.where(incl, qk2 * dec, 0.0).astype(BF16),
                rhs=jnp.concatenate([vst * bst, kst * bst * eg], axis=1).astype(BF16),
                qg=(qst * eg).astype(BF16),
                kjb=kj.astype(BF16),
                e_st=jnp.exp(jnp.concatenate([gl0 - g0, gl1 - g1], axis=0)),
                dl0=jnp.exp(gl0), dl1=jnp.exp(gl1))
        t_invs = _neumann_inverse([-lp for lp in lps], eye)
        for (c, j), t_inv in zip(combos, t_invs):
            uw = _dot(t_inv.astype(BF16), pre[(c, j)]["rhs"])
            pre[(c, j)]["u"] = uw[:, :128]
            pre[(c, j)]["w"] = uw[:, 128:].astype(BF16)

        gs = [None] * GDN_HV
        for c in range(n_chunks):
            s = c // chunks_per_seq
            if c % chunks_per_seq == 0:
                for h in range(GDN_HV):
                    gs[h] = o_gdn[0, h] if prompt else gdn_s[s, h]
            rr = []
            for j in range(4):
                p_ = pre[(c, j)]
                for hh in range(2):
                    half = slice(hh * CHUNK, (hh + 1) * CHUNK)
                    rr.append(_dot(jnp.concatenate([p_["w"][half], p_["qg"][half]], axis=0),
                                   gs[2 * j + hh].astype(BF16)))
            vnews = [pre[(c, j)]["u"] - jnp.concatenate([rr[2 * j][:CHUNK], rr[2 * j + 1][:CHUNK]], axis=0)
                     for j in range(4)]
            upds = []
            for j in range(4):
                vn_e = vnews[j] * pre[(c, j)]["e_st"]
                upds.append(_dot_tn(pre[(c, j)]["kjb"],
                                    jnp.concatenate([vn_e[:CHUNK], vn_e[CHUNK:]], axis=1).astype(BF16)))
            for j in range(4):
                h0, h1 = 2 * j, 2 * j + 1
                gs[h0] = pre[(c, j)]["dl0"] * gs[h0] + upds[j][:, :128]
                gs[h1] = pre[(c, j)]["dl1"] * gs[h1] + upds[j][:, 128:]
            yr = slice(r0 + c * CHUNK, r0 + (c + 1) * CHUNK)
            for j in range(4):
                h0, h1 = 2 * j, 2 * j + 1
                o_st = (jnp.concatenate([rr[2 * j][CHUNK:], rr[2 * j + 1][CHUNK:]], axis=0)
                        + _dot(pre[(c, j)]["ap"], vnews[j].astype(BF16)))
                ygdn[yr, 128 * h0:128 * (h0 + 1)] = o_st[:CHUNK]
                ygdn[yr, 128 * h1:128 * (h1 + 1)] = o_st[CHUNK:]
            if (c + 1) % chunks_per_seq == 0:
                for h in range(GDN_HV):
                    o_gdn[s, h] = gs[h]
        og = ygdn[rows, :]
        og = jnp.concatenate(
            [og[:, 128 * h:128 * (h + 1)]
             * lax.rsqrt(jnp.mean(og[:, 128 * h:128 * (h + 1)] ** 2, axis=-1, keepdims=True) + RMS_EPS)
             for h in range(GDN_HV)], axis=1)
        y_gdn = (og * gdnnw[...] * _silu(z_gdn)).astype(BF16)

        gates = jax.nn.sigmoid(jnp.concatenate([gate_a, gate_b], axis=1))
        merged = (gates[:, :D_MODEL] * br_pool
                  + gates[:, D_MODEL:2 * D_MODEL] * _dot(y_ssd, wbr[1])
                  + gates[:, 2 * D_MODEL:] * _dot(y_gdn, wbr[2]))
        xn = x + _dot(merged.astype(BF16), wout[...])
        if final_norm:
            xn = xn * lax.rsqrt(jnp.mean(xn * xn, axis=-1, keepdims=True) + RMS_EPS) * fnw[...]
        xo[rows, :] = xn

    for sub in range(n_sub):
        sub_block(sub)

    if prompt:
        abuf[0, 0:HEAD_ROWS, :] = abuf[0, tb:tb + HEAD_ROWS, :]
        sbuf[0, 0:HEAD_ROWS, :] = sbuf[0, tb:tb + HEAD_ROWS, :]
        gbuf[0, 0:HEAD_ROWS, :] = gbuf[0, tb:tb + HEAD_ROWS, :]

        @pl.when(step == pl.num_programs(0) - 1)
        def _final_state():
            o_ssd[0] = st_scr[...].T


def _layer_spec(arr, layer):
    nd = arr.ndim - 1
    return pl.BlockSpec((None,) + arr.shape[1:], lambda i, _l=layer, _nd=nd: (_l,) + (0,) * _nd,
                        pipeline_mode=pl.Buffered(1))


def _const_spec(arr):
    nd = arr.ndim
    return pl.BlockSpec(arr.shape, lambda i, _nd=nd: (0,) * _nd, pipeline_mode=pl.Buffered(1))


def _run_layer(layer, x2d, states, weights, fnw, consts, stacked, *, prompt, final_norm):
    tokens = x2d.shape[0]
    if prompt:
        n_seq, seq_len, n_state, n_sub = 1, PROMPT_BLOCK, 1, PROMPT_SUB_BLOCKS
    else:
        n_seq, seq_len, n_state, n_sub = SAMPLE_SEQS, CHUNK, tokens // CHUNK, 1
    tb = n_seq * seq_len
    grid = (tokens // tb,)

    def state_spec(shape):
        nd = len(shape)
        if prompt:
            return pl.BlockSpec((None, 1) + shape, lambda i, _nd=nd: (layer, 0) + (0,) * _nd)
        return pl.BlockSpec((None, n_seq) + shape, lambda i, _nd=nd: (layer, i) + (0,) * _nd)

    in_specs = [pl.BlockSpec((tb, D_MODEL), lambda i: (i, 0))]
    args = [x2d]
    if not prompt:
        in_specs += [state_spec(s) for s in STATE_SHAPES]
        args += list(states)
    in_specs += [_layer_spec(w, layer) for w in weights] + [_const_spec(fnw)] + [_const_spec(c) for c in consts]
    args += list(weights) + [fnw] + list(consts)
    aliases = {}
    for k, buf in enumerate(stacked):
        aliases[len(args)] = 1 + k
        in_specs.append(pl.BlockSpec(memory_space=pl.ANY))
        args.append(buf)

    out_shape = [jax.ShapeDtypeStruct((tokens, D_MODEL), F32)]
    out_shape += [jax.ShapeDtypeStruct((DEPTH, n_state) + s, F32) for s in STATE_SHAPES]
    out_specs = [pl.BlockSpec((tb, D_MODEL), lambda i: (i, 0))] + [state_spec(s) for s in STATE_SHAPES]

    scratch = [pltpu.VMEM((n_seq, HEAD_ROWS + seq_len, D_MODEL), F32),
               pltpu.VMEM((n_seq, HEAD_ROWS + seq_len, SSD_CONV), F32),
               pltpu.VMEM((n_seq, HEAD_ROWS + seq_len, GDN_CONV), F32),
               pltpu.VMEM((tb, D_MODEL), F32),
               pltpu.VMEM((tb, D_MODEL), F32),
               pltpu.VMEM((SSD_N, D_MODEL), F32),
               pltpu.VMEM((tb, D_MODEL), BF16)]
    kern = functools.partial(_layer_kernel, prompt=prompt, n_seq=n_seq, seq_len=seq_len, n_sub=n_sub,
                             final_norm=final_norm, n_alias=len(aliases))
    return pl.pallas_call(
        kern, grid=grid, in_specs=in_specs, out_specs=out_specs, out_shape=out_shape,
        scratch_shapes=scratch, input_output_aliases=aliases,
        compiler_params=pltpu.CompilerParams(dimension_semantics=("arbitrary",),
                                             vmem_limit_bytes=VMEM_LIMIT_BYTES),
    )(*args)


def _expand_matrix(first_lane, n_heads, width):
    rows = jnp.arange(SMALL_W)
    src = rows % 32
    head = src - first_lane
    valid = (rows < 96) & (head >= 0) & (head < n_heads)
    cols = jnp.arange(n_heads * width) // width
    return (valid[:, None] & (head[:, None] == cols[None, :])).astype(BF16)


def _block_tri(tb):
    i = jnp.arange(tb)
    return ((i[:, None] // CHUNK == i[None, :] // CHUNK) & (i[None, :] <= i[:, None])).astype(BF16)


def _all_layer_weights(norm_w, w_in, pool_w, pool_scale, ssd_conv_w, ssd_conv_b, ssd_dt_bias, ssd_A_log,
                       ssd_D, ssd_norm_w, gdn_conv_w, gdn_dt_bias, gdn_A_log, gdn_norm_w, w_br_pool,
                       w_br_ssd, w_br_gdn, w_out):
    n_pad = SMALL_W - SSD_H - 2 * GDN_HV
    w_a = w_in[:, :, 0:IN_DT].astype(BF16)
    w_gdn = w_in[:, :, IN_GDN:IN_BETA].astype(BF16)
    w_gate = w_in[:, :, IN_GATE:IN_END].astype(BF16)
    w_small = jnp.concatenate([w_in[:, :, IN_DT:IN_GDN], w_in[:, :, IN_BETA:IN_GATE],
                               jnp.zeros((DEPTH, D_MODEL, n_pad), F32)], axis=2).astype(BF16)
    pad8 = jnp.zeros((DEPTH, GDN_HV), F32)
    pad = jnp.zeros((DEPTH, n_pad), F32)
    sbias = jnp.concatenate([ssd_dt_bias, pad8, gdn_dt_bias, pad], axis=1)[:, None, :]
    salog = jnp.concatenate([ssd_A_log, pad8, gdn_A_log, pad], axis=1)[:, None, :]
    return [
        norm_w[:, None, :], w_a, w_small, w_gdn, w_gate, pool_w.astype(BF16), pool_scale[:, None, :],
        ssd_conv_w, ssd_conv_b[:, None, :], sbias, salog,
        jnp.repeat(ssd_D, SSD_P, axis=1)[:, None, :], ssd_norm_w[:, None, :],
        gdn_conv_w, jnp.tile(gdn_norm_w, (1, GDN_HV))[:, None, :],
        jnp.stack([w_br_pool, w_br_ssd, w_br_gdn], axis=1).astype(BF16), w_out.astype(BF16),
    ]


def kernel(x_prompt, x_sample, state_pool, state_ssd_conv, state_ssd, state_gdn_conv, state_gdn, norm_w, w_in,
           pool_w, pool_scale, ssd_conv_w, ssd_conv_b, ssd_dt_bias, ssd_A_log, ssd_D, ssd_norm_w, gdn_conv_w,
           gdn_dt_bias, gdn_A_log, gdn_norm_w, w_br_pool, w_br_ssd, w_br_gdn, w_out, final_norm_w):
    bp, seq, _ = x_prompt.shape
    bs, dseq, _ = x_sample.shape
    assert bp == 1 and dseq == CHUNK and seq % PROMPT_BLOCK == 0 and bs % SAMPLE_SEQS == 0
    e_mats = [_expand_matrix(0, SSD_H, SSD_P), _expand_matrix(SSD_H, GDN_HV, GDN_DK),
              _expand_matrix(SSD_H + GDN_HV, GDN_HV, GDN_DK)]
    consts_p = [_block_tri(PROMPT_BLOCK // PROMPT_SUB_BLOCKS)] + e_mats
    consts_s = [_block_tri(SAMPLE_SEQS * CHUNK)] + e_mats
    weights = _all_layer_weights(norm_w, w_in, pool_w, pool_scale, ssd_conv_w, ssd_conv_b, ssd_dt_bias,
                                 ssd_A_log, ssd_D, ssd_norm_w, gdn_conv_w, gdn_dt_bias, gdn_A_log, gdn_norm_w,
                                 w_br_pool, w_br_ssd, w_br_gdn, w_out)
    fnw = final_norm_w[None, :]
    states_s = (state_pool, state_ssd_conv, state_ssd.reshape(DEPTH, bs, D_MODEL, SSD_N),
                state_gdn_conv, state_gdn)

    hp = x_prompt.reshape(seq, D_MODEL)
    hs = x_sample.reshape(bs * dseq, D_MODEL)
    stk_p = [jnp.zeros((DEPTH, 1) + s, F32) for s in STATE_SHAPES]
    stk_s = [jnp.zeros((DEPTH, bs) + s, F32) for s in STATE_SHAPES]
    for l in range(DEPTH):
        last = l == DEPTH - 1
        rp = _run_layer(l, hp, None, weights, fnw, consts_p, stk_p, prompt=True, final_norm=last)
        rs = _run_layer(l, hs, states_s, weights, fnw, consts_s, stk_s, prompt=False, final_norm=last)
        hp, stk_p = rp[0], rp[1:]
        hs, stk_s = rs[0], rs[1:]

    def ssd_out(st):
        return st.reshape(DEPTH, st.shape[1], SSD_H, SSD_P, SSD_N)

    return (hp.reshape(bp, seq, D_MODEL), hs.reshape(bs, dseq, D_MODEL),
            stk_p[0], stk_s[0], stk_p[1], stk_s[1], ssd_out(stk_p[2]), ssd_out(stk_s[2]),
            stk_p[3], stk_s[3], stk_p[4], stk_s[4])
```

```python
import functools

import jax
import jax.numpy as jnp
from jax import lax
from jax.experimental import pallas as pl
from jax.experimental.pallas import tpu as pltpu

F32 = jnp.float32
BF16 = jnp.bfloat16

D_MODEL = 1024
DEPTH = 4
CHUNK = 64
RMS_EPS = 1e-6
POOL_HIST = 15
CONV_HIST = 3
HEAD_ROWS = 16
SSD_H, SSD_P, SSD_N, SSD_G = 16, 64, 128, 2
SSD_CONV = 1536
GDN_HV, GDN_DK, GDN_DV = 8, 128, 128
GDN_CONV = 2048
SMALL_W = 128

W_POOL, W_SSD = 2048, 2560
IN_DT, IN_GDN, IN_BETA, IN_GATE, IN_END = 4608, 4624, 7696, 7712, 10784

PROMPT_BLOCK = 256
PROMPT_SUB_BLOCKS = 1
SAMPLE_SEQS = 2
VMEM_LIMIT_BYTES = 60 * 1024 * 1024

STATE_SHAPES = ((POOL_HIST, D_MODEL), (CONV_HIST, SSD_CONV), (D_MODEL, SSD_N),
                (CONV_HIST, GDN_CONV), (GDN_HV, GDN_DK, GDN_DV))


def _dot(a, b):
    return jnp.dot(a, b, preferred_element_type=F32)


def _dot_nt(a, b):
    return lax.dot_general(a, b, (((1,), (1,)), ((), ())), preferred_element_type=F32)


def _dot_tn(a, b):
    return lax.dot_general(a, b, (((0,), (0,)), ((), ())), preferred_element_type=F32)


def _silu(x):
    return x * jax.nn.sigmoid(x)


def _softplus(x):
    return jnp.maximum(x, 0.0) + jnp.log1p(jnp.exp(-jnp.abs(x)))


def _split3(a):
    hi = a.astype(BF16)
    r = a - hi.astype(F32)
    mid = r.astype(BF16)
    lo = (r - mid.astype(F32)).astype(BF16)
    return hi, mid, lo


def _pack3(a, lane_lt32):
    hi, mid, lo = _split3(a)
    z = jnp.zeros_like(a)
    p = (jnp.where(lane_lt32, hi.astype(F32), z)
         + pltpu.roll(jnp.where(lane_lt32, mid.astype(F32), z), 32, 1)
         + pltpu.roll(jnp.where(lane_lt32, lo.astype(F32), z), 64, 1))
    return p.astype(BF16)


def _causal_conv(buf, s, row, length, w, k_taps):
    acc = buf[s, row:row + length, :] * w[k_taps - 1:k_taps, :]
    for j in range(1, k_taps):
        acc = acc + buf[s, row - j:row - j + length, :] * w[k_taps - 1 - j:k_taps - j, :]
    return acc


def _after(a_bf16, dep):
    u = pltpu.bitcast(dep, jnp.uint32)
    zero = lax.shift_right_logical(lax.shift_right_logical(u, jnp.uint32(16)), jnp.uint32(16))
    bits = pltpu.bitcast(a_bf16, jnp.uint32)
    reps = (bits.shape[0] // zero.shape[0], bits.shape[1] // zero.shape[1])
    return pltpu.bitcast(bits | jnp.tile(zero, reps), BF16)


def _neumann_inverse(xs, eye):
    ps = [eye + x for x in xs]
    xbs = [x.astype(BF16) for x in xs]
    ys = [_dot(xb, xb) for xb in xbs]
    for _ in range(4):
        ybs = [y.astype(BF16) for y in ys]
        rs = [_dot(jnp.concatenate([yb, p.astype(BF16)], axis=0), yb) for yb, p in zip(ybs, ps)]
        ys = [r[:128] for r in rs]
        ps = [p + r[128:] for p, r in zip(ps, rs)]
    return [p + _dot(p.astype(BF16), y.astype(BF16)) for p, y in zip(ps, ys)]


def _layer_kernel(*refs, prompt, n_seq, seq_len, n_sub, final_norm, n_alias):
    tb = n_seq * seq_len
    tbs = tb // n_sub
    n_chunks = tbs // CHUNK
    it = iter(refs)
    x_ref = next(it)
    if not prompt:
        pool_h, sconv_h, ssd_s, gconv_h, gdn_s = (next(it) for _ in range(5))
    (normw, w_a, w_small, w_gdn, w_gate, poolw, poolsc, sconvw, sconvb, sbias, salog, ssd_d, ssdnw, gconvw,
     gdnnw, wbr, wout, fnw, btri, e_dt, e_beta, e_g) = (next(it) for _ in range(22))
    for _ in range(n_alias):
        next(it)
    xo, o_pool, o_sconv, o_ssd, o_gconv, o_gdn = (next(it) for _ in range(6))
    abuf, sbuf, gbuf, yssd, ygdn, st_scr, hn_scr = (next(it) for _ in range(7))

    step = pl.program_id(0)
    if prompt:
        @pl.when(step == 0)
        def _init():
            abuf[:, 0:HEAD_ROWS, :] = jnp.zeros((n_seq, HEAD_ROWS, D_MODEL), F32)
            sbuf[:, 0:HEAD_ROWS, :] = jnp.zeros((n_seq, HEAD_ROWS, SSD_CONV), F32)
            gbuf[:, 0:HEAD_ROWS, :] = jnp.zeros((n_seq, HEAD_ROWS, GDN_CONV), F32)
            st_scr[...] = jnp.zeros(st_scr.shape, F32)
            o_gdn[...] = jnp.zeros(o_gdn.shape, F32)
    else:
        abuf[:, 0:HEAD_ROWS, :] = jnp.zeros((n_seq, HEAD_ROWS, D_MODEL), F32)
        sbuf[:, 0:HEAD_ROWS, :] = jnp.zeros((n_seq, HEAD_ROWS, SSD_CONV), F32)
        gbuf[:, 0:HEAD_ROWS, :] = jnp.zeros((n_seq, HEAD_ROWS, GDN_CONV), F32)
        abuf[:, HEAD_ROWS - POOL_HIST:HEAD_ROWS, :] = pool_h[...]
        sbuf[:, HEAD_ROWS - CONV_HIST:HEAD_ROWS, :] = sconv_h[...]
        gbuf[:, HEAD_ROWS - CONV_HIST:HEAD_ROWS, :] = gconv_h[...]

    row16 = lax.broadcasted_iota(jnp.int32, (HEAD_ROWS, 256), 0).astype(F32)
    lane = lax.broadcasted_iota(jnp.int32, (1, SMALL_W), 1)
    is_decay = (lane < SSD_H) | ((lane >= SSD_H + GDN_HV) & (lane < SSD_H + 2 * GDN_HV))
    lane_lt32 = lane < 32
    q_idx = lax.broadcasted_iota(jnp.int32, (CHUNK, D_MODEL), 0)
    s_idx = lax.broadcasted_iota(jnp.int32, (CHUNK, D_MODEL), 1) & (CHUNK - 1)
    causal = s_idx <= q_idx
    diag_sel = (s_idx == q_idx).astype(F32)
    ri = lax.broadcasted_iota(jnp.int32, (128, 128), 0)
    ci = lax.broadcasted_iota(jnp.int32, (128, 128), 1)
    same = (ri >> 6) == (ci >> 6)
    incl = same & ((ci & 63) <= (ri & 63))
    strict = same & ((ci & 63) < (ri & 63))
    eye = (ri == ci).astype(F32)
    q64 = lax.broadcasted_iota(jnp.int32, (CHUNK, 128), 0)
    l64 = lax.broadcasted_iota(jnp.int32, (CHUNK, 128), 1)
    diag_sel2 = ((l64 & 63) == q64).astype(F32)
    lo_half64 = l64 < CHUNK

    def l2n(t):
        return t * lax.rsqrt(jnp.sum(t * t, axis=-1, keepdims=True) + 1e-6)

    def sub_block(sub):
        r0 = sub * tbs
        rows = slice(r0, r0 + tbs)
        if prompt:
            seqs = [(0, r0, 0, tbs)]
        else:
            seqs = [(s, 0, s * seq_len, seq_len) for s in range(n_seq)]
        chunks_per_seq = seqs[0][3] // CHUNK

        x = x_ref[rows, :]
        hn_scr[rows, :] = (x * lax.rsqrt(jnp.mean(x * x, axis=-1, keepdims=True) + RMS_EPS)
                           * normw[...]).astype(BF16)
        pp = _dot(hn_scr[rows, :], w_a[:, 0:W_POOL])
        sm = _dot(hn_scr[rows, :], w_small[...])

        a_in, z_pool = pp[:, :D_MODEL], pp[:, D_MODEL:]
        y_pool_parts = []
        for s, b0, x0, ln in seqs:
            abuf[s, b0 + HEAD_ROWS:b0 + HEAD_ROWS + ln, :] = a_in[x0:x0 + ln]
            full = abuf[s, b0:b0 + HEAD_ROWS + ln, :]
            o_pool[s] = full[ln + HEAD_ROWS - POOL_HIST:, :]
            ys = []
            for g in range(4):
                win = 2 << g
                fg = full[:, 256 * g:256 * (g + 1)]
                acc = fg
                for k in range(g + 1):
                    acc = acc + pltpu.roll(acc, 1 << k, 0)
                wsum = acc[HEAD_ROWS:]
                cur = fg[HEAD_ROWS:]
                inv_w = 1.0 / win
                if prompt and sub == 0:
                    fac = jnp.where(step == 0, 1.0 / jnp.minimum(row16 + 1.0, float(win)), inv_w)
                    d = jnp.concatenate([wsum[:HEAD_ROWS] * fac - cur[:HEAD_ROWS],
                                         wsum[HEAD_ROWS:] * inv_w - cur[HEAD_ROWS:]], axis=0)
                else:
                    d = wsum * inv_w - cur
                ys.append(_dot(d.astype(BF16), poolw[g]))
            y_pool_parts.append(jnp.concatenate(ys, axis=1))
        y_pool = jnp.concatenate(y_pool_parts, axis=0) if len(seqs) > 1 else y_pool_parts[0]
        y_pool = (y_pool * poolsc[...] * _silu(z_pool)).astype(BF16)

        sp = _softplus(sm + sbias[...])
        sg = jax.nn.sigmoid(sm)
        av = sp * jnp.where(is_decay, -jnp.exp(salog[...]), 0.0)
        a_hi, a_mid, a_lo = _split3(av)
        bt = btri[...]
        acum = _dot(bt, a_hi) + _dot(bt, a_mid) + _dot(bt, a_lo)
        pk_ac = _pack3(acum, lane_lt32)
        dt_exp = _dot(_pack3(sp, lane_lt32), e_dt[...])
        acum_exp = _dot(pk_ac, e_dt[...])
        beta128 = _dot(_pack3(sg, lane_lt32), e_beta[...])
        gc128 = _dot(pk_ac, e_g[...])

        ps = _dot(hn_scr[rows, :], w_a[:, W_POOL:W_POOL + W_SSD])
        u_ssd = jnp.concatenate([ps[:, :D_MODEL], ps[:, 2 * D_MODEL:]], axis=1)
        z_ssd = ps[:, D_MODEL:2 * D_MODEL]
        gate_mid = 3 * D_MODEL // 2
        gate_a = _dot(_after(hn_scr[rows, :], ps[0:8, 0:128]), w_gate[:, 0:gate_mid])
        acts = []
        for s, b0, x0, ln in seqs:
            sbuf[s, b0 + HEAD_ROWS:b0 + HEAD_ROWS + ln, :] = u_ssd[x0:x0 + ln]
            o_sconv[s] = u_ssd[x0 + ln - CONV_HIST:x0 + ln]
            acts.append(_silu(_causal_conv(sbuf, s, b0 + HEAD_ROWS, ln, sconvw[...], 4) + sconvb[...]))
        xbc = jnp.concatenate(acts, axis=0) if len(seqs) > 1 else acts[0]
        xc, bm, cm_ = xbc[:, :D_MODEL], xbc[:, D_MODEL:D_MODEL + 256], xbc[:, D_MODEL + 256:]
        br_pool = _dot(_after(y_pool, xc[0:8, 0:128]), wbr[0])

        cr = [slice(c * CHUNK, (c + 1) * CHUNK) for c in range(n_chunks)]
        acs = [acum_exp[r] for r in cr]
        a_rows = [jnp.sum(ac * diag_sel, axis=0, keepdims=True) for ac in acs]
        l_alls = [jnp.where(causal, jnp.exp(ac - a_row), 0.0) for ac, a_row in zip(acs, a_rows)]
        bcs = [bm[r].astype(BF16) for r in cr]
        ccs = [cm_[r].astype(BF16) for r in cr]
        cbts = [jnp.concatenate(
            [_dot_nt(cc[:, 128 * g:128 * (g + 1)], jnp.concatenate([bc[:, 128 * g:128 * (g + 1)]] * 8, axis=0))
             for g in range(SSD_G)], axis=1) for bc, cc in zip(bcs, ccs)]
        m_alls = [(cbt * l_all).astype(BF16) for cbt, l_all in zip(cbts, l_alls)]
        xdts = [xc[r] * dt_exp[r] for r in cr]
        zb = jnp.zeros((CHUNK, 128), BF16)
        y_intras = []
        for m_all, xdt in zip(m_alls, xdts):
            xdtb = xdt.astype(BF16)
            y_parts = []
            for j in range(8):
                xp = xdtb[:, 128 * j:128 * (j + 1)]
                rhs = jnp.concatenate([jnp.where(lo_half64, xp, zb), jnp.where(lo_half64, zb, xp)], axis=0)
                y_parts.append(_dot(m_all[:, 128 * j:128 * (j + 1)], rhs))
            y_intras.append(jnp.concatenate(y_parts, axis=1))
        a_lasts = [ac[CHUNK - 1:CHUNK, :] for ac in acs]
        xds = [(xdt * jnp.exp(a_last - ac)).astype(BF16) for xdt, a_last, ac in zip(xdts, a_lasts, acs)]
        upds = [jnp.concatenate(
            [_dot_tn(bc[:, 128 * g:128 * (g + 1)], xd[:, 512 * g:512 * (g + 1)]) for g in range(SSD_G)], axis=1)
            for bc, xd in zip(bcs, xds)]

        st = None
        for c in range(n_chunks):
            r = cr[c]
            s = c // chunks_per_seq
            if c % chunks_per_seq == 0:
                st = st_scr[...] if prompt else ssd_s[s].T
            stb = st.astype(BF16)
            y = y_intras[c] + jnp.exp(acs[c]) * jnp.concatenate(
                [_dot(ccs[c][:, 128 * g:128 * (g + 1)], stb[:, 512 * g:512 * (g + 1)])
                 for g in range(SSD_G)], axis=1)
            st = jnp.exp(a_lasts[c]) * st + upds[c]
            if (c + 1) % chunks_per_seq == 0:
                if prompt:
                    st_scr[...] = st
                else:
                    o_ssd[s] = st.T
            y = y + ssd_d[...] * xc[r]
            yz = y * _silu(z_ssd[r])
            yn = jnp.concatenate(
                [yz[:, 512 * g:512 * (g + 1)]
                 * lax.rsqrt(jnp.mean(yz[:, 512 * g:512 * (g + 1)] ** 2, axis=-1, keepdims=True) + RMS_EPS)
                 for g in range(SSD_G)], axis=1)
            yssd[r0 + c * CHUNK:r0 + (c + 1) * CHUNK, :] = yn * ssdnw[...]
        y_ssd = yssd[rows, :].astype(BF16)

        pg = _dot(_after(hn_scr[rows, :], xc[8:16, 0:128]), w_gdn[...])
        u_gdn, z_gdn = pg[:, :GDN_CONV], pg[:, GDN_CONV:]
        gate_b = _dot(_after(hn_scr[rows, :], pg[0:8, 0:128]), w_gate[:, gate_mid:])
        acts = []
        for s, b0, x0, ln in seqs:
            gbuf[s, b0 + HEAD_ROWS:b0 + HEAD_ROWS + ln, :] = u_gdn[x0:x0 + ln]
            o_gconv[s] = u_gdn[x0 + ln - CONV_HIST:x0 + ln]
            acts.append(_silu(_causal_conv(gbuf, s, b0 + HEAD_ROWS, ln, gconvw[...], 4)))
        qkv = jnp.concatenate(acts, axis=0) if len(seqs) > 1 else acts[0]

        qn = [l2n(qkv[:, 128 * j:128 * (j + 1)]) * (GDN_DK ** -0.5) for j in range(4)]
        kn = [l2n(qkv[:, 512 + 128 * j:512 + 128 * (j + 1)]) for j in range(4)]
        vc = qkv[:, 1024:]

        combos = [(c, j) for c in range(n_chunks) for j in range(4)]
        pre = {}
        kqs = [_dot_nt(jnp.concatenate([kn[j][c * CHUNK:(c + 1) * CHUNK], qn[j][c * CHUNK:(c + 1) * CHUNK]],
                                       axis=0).astype(BF16),
                       jnp.concatenate([kn[j][c * CHUNK:(c + 1) * CHUNK]] * 2, axis=0).astype(BF16))
               for c, j in combos]
        lps = []
        for (c, j), kq in zip(combos, kqs):
            r = slice(c * CHUNK, (c + 1) * CHUNK)
            h0, h1 = 2 * j, 2 * j + 1
            kj, qj = kn[j][r], qn[j][r]
            g0 = gc128[r, 128 * h0:128 * (h0 + 1)]
            g1 = gc128[r, 128 * h1:128 * (h1 + 1)]
            kst = jnp.concatenate([kj, kj], axis=0)
            qst = jnp.concatenate([qj, qj], axis=0)
            bst = jnp.concatenate([beta128[r, 128 * h0:128 * (h0 + 1)],
                                   beta128[r, 128 * h1:128 * (h1 + 1)]], axis=0)
            gst = jnp.concatenate([g0, g1], axis=0)
            vst = jnp.concatenate([vc[r, 128 * h0:128 * (h0 + 1)], vc[r, 128 * h1:128 * (h1 + 1)]], axis=0)
            g_row = jnp.sum(jnp.where(lo_half64, g0, g1) * diag_sel2, axis=0, keepdims=True)
            dec = jnp.exp(gst - g_row)
            eg = jnp.exp(gst)
            kk2 = jnp.concatenate([kq[:CHUNK], kq[:CHUNK]], axis=0)
            qk2 = jnp.concatenate([kq[CHUNK:], kq[CHUNK:]], axis=0)
            lps.append(jnp.where(strict, kk2 * bst * dec, 0.0))
            gl0, gl1 = g0[CHUNK - 1:CHUNK, :], g1[CHUNK - 1:CHUNK, :]
            pre[(c, j)] = dict(
                ap=jnp.where(incl, qk2 * dec, 0.0).astype(BF16),
                rhs=jnp.concatenate([vst * bst, kst * bst * eg], axis=1).astype(BF16),
                qg=(qst * eg).astype(BF16),
                kjb=kj.astype(BF16),
                e_st=jnp.exp(jnp.concatenate([gl0 - g0, gl1 - g1], axis=0)),
                dl0=jnp.exp(gl0), dl1=jnp.exp(gl1))
        t_invs = _neumann_inverse([-lp for lp in lps], eye)
        for (c, j), t_inv in zip(combos, t_invs):
            uw = _dot(t_inv.astype(BF16), pre[(c, j)]["rhs"])
            pre[(c, j)]["u"] = uw[:, :128]
            pre[(c, j)]["w"] = uw[:, 128:].astype(BF16)

        gs = [None] * GDN_HV
        for c in range(n_chunks):
            s = c // chunks_per_seq
            if c % chunks_per_seq == 0:
                for h in range(GDN_HV):
                    gs[h] = o_gdn[0, h] if prompt else gdn_s[s, h]
            rr = []
            for j in range(4):
                p_ = pre[(c, j)]
                for hh in range(2):
                    half = slice(hh * CHUNK, (hh + 1) * CHUNK)
                    rr.append(_dot(jnp.concatenate([p_["w"][half], p_["qg"][half]], axis=0),
                                   gs[2 * j + hh].astype(BF16)))
            vnews = [pre[(c, j)]["u"] - jnp.concatenate([rr[2 * j][:CHUNK], rr[2 * j + 1][:CHUNK]], axis=0)
                     for j in range(4)]
            upds = []
            for j in range(4):
                vn_e = vnews[j] * pre[(c, j)]["e_st"]
                upds.append(_dot_tn(pre[(c, j)]["kjb"],
                                    jnp.concatenate([vn_e[:CHUNK], vn_e[CHUNK:]], axis=1).astype(BF16)))
            for j in range(4):
                h0, h1 = 2 * j, 2 * j + 1
                gs[h0] = pre[(c, j)]["dl0"] * gs[h0] + upds[j][:, :128]
                gs[h1] = pre[(c, j)]["dl1"] * gs[h1] + upds[j][:, 128:]
            yr = slice(r0 + c * CHUNK, r0 + (c + 1) * CHUNK)
            for j in range(4):
                h0, h1 = 2 * j, 2 * j + 1
                o_st = (jnp.concatenate([rr[2 * j][CHUNK:], rr[2 * j + 1][CHUNK:]], axis=0)
                        + _dot(pre[(c, j)]["ap"], vnews[j].astype(BF16)))
                ygdn[yr, 128 * h0:128 * (h0 + 1)] = o_st[:CHUNK]
                ygdn[yr, 128 * h1:128 * (h1 + 1)] = o_st[CHUNK:]
            if (c + 1) % chunks_per_seq == 0:
                for h in range(GDN_HV):
                    o_gdn[s, h] = gs[h]
        og = ygdn[rows, :]
        og = jnp.concatenate(
            [og[:, 128 * h:128 * (h + 1)]
             * lax.rsqrt(jnp.mean(og[:, 128 * h:128 * (h + 1)] ** 2, axis=-1, keepdims=True) + RMS_EPS)
             for h in range(GDN_HV)], axis=1)
        y_gdn = (og * gdnnw[...] * _silu(z_gdn)).astype(BF16)

        gates = jax.nn.sigmoid(jnp.concatenate([gate_a, gate_b], axis=1))
        merged = (gates[:, :D_MODEL] * br_pool
                  + gates[:, D_MODEL:2 * D_MODEL] * _dot(y_ssd, wbr[1])
                  + gates[:, 2 * D_MODEL:] * _dot(y_gdn, wbr[2]))
        xn = x + _dot(merged.astype(BF16), wout[...])
        if final_norm:
            xn = xn * lax.rsqrt(jnp.mean(xn * xn, axis=-1, keepdims=True) + RMS_EPS) * fnw[...]
        xo[rows, :] = xn

    for sub in range(n_sub):
        sub_block(sub)

    if prompt:
        abuf[0, 0:HEAD_ROWS, :] = abuf[0, tb:tb + HEAD_ROWS, :]
        sbuf[0, 0:HEAD_ROWS, :] = sbuf[0, tb:tb + HEAD_ROWS, :]
        gbuf[0, 0:HEAD_ROWS, :] = gbuf[0, tb:tb + HEAD_ROWS, :]

        @pl.when(step == pl.num_programs(0) - 1)
        def _final_state():
            o_ssd[0] = st_scr[...].T


def _layer_spec(arr, layer):
    nd = arr.ndim - 1
    return pl.BlockSpec((None,) + arr.shape[1:], lambda i, _l=layer, _nd=nd: (_l,) + (0,) * _nd,
                        pipeline_mode=pl.Buffered(1))


def _const_spec(arr):
    nd = arr.ndim
    return pl.BlockSpec(arr.shape, lambda i, _nd=nd: (0,) * _nd, pipeline_mode=pl.Buffered(1))


def _run_layer(layer, x2d, states, weights, fnw, consts, stacked, *, prompt, final_norm):
    tokens = x2d.shape[0]
    if prompt:
        n_seq, seq_len, n_state, n_sub = 1, PROMPT_BLOCK, 1, PROMPT_SUB_BLOCKS
    else:
        n_seq, seq_len, n_state, n_sub = SAMPLE_SEQS, CHUNK, tokens // CHUNK, 1
    tb = n_seq * seq_len
    grid = (tokens // tb,)

    def state_spec(shape):
        nd = len(shape)
        if prompt:
            return pl.BlockSpec((None, 1) + shape, lambda i, _nd=nd: (layer, 0) + (0,) * _nd)
        return pl.BlockSpec((None, n_seq) + shape, lambda i, _nd=nd: (layer, i) + (0,) * _nd)

    in_specs = [pl.BlockSpec((tb, D_MODEL), lambda i: (i, 0))]
    args = [x2d]
    if not prompt:
        in_specs += [state_spec(s) for s in STATE_SHAPES]
        args += list(states)
    in_specs += [_layer_spec(w, layer) for w in weights] + [_const_spec(fnw)] + [_const_spec(c) for c in consts]
    args += list(weights) + [fnw] + list(consts)
    aliases = {}
    for k, buf in enumerate(stacked):
        aliases[len(args)] = 1 + k
        in_specs.append(pl.BlockSpec(memory_space=pl.ANY))
        args.append(buf)

    out_shape = [jax.ShapeDtypeStruct((tokens, D_MODEL), F32)]
    out_shape += [jax.ShapeDtypeStruct((DEPTH, n_state) + s, F32) for s in STATE_SHAPES]
    out_specs = [pl.BlockSpec((tb, D_MODEL), lambda i: (i, 0))] + [state_spec(s) for s in STATE_SHAPES]

    scratch = [pltpu.VMEM((n_seq, HEAD_ROWS + seq_len, D_MODEL), F32),
               pltpu.VMEM((n_seq, HEAD_ROWS + seq_len, SSD_CONV), F32),
               pltpu.VMEM((n_seq, HEAD_ROWS + seq_len, GDN_CONV), F32),
               pltpu.VMEM((tb, D_MODEL), F32),
               pltpu.VMEM((tb, D_MODEL), F32),
               pltpu.VMEM((SSD_N, D_MODEL), F32),
               pltpu.VMEM((tb, D_MODEL), BF16)]
    kern = functools.partial(_layer_kernel, prompt=prompt, n_seq=n_seq, seq_len=seq_len, n_sub=n_sub,
                             final_norm=final_norm, n_alias=len(aliases))
    return pl.pallas_call(
        kern, grid=grid, in_specs=in_specs, out_specs=out_specs, out_shape=out_shape,
        scratch_shapes=scratch, input_output_aliases=aliases,
        compiler_params=pltpu.CompilerParams(dimension_semantics=("arbitrary",),
                                             vmem_limit_bytes=VMEM_LIMIT_BYTES),
    )(*args)


def _expand_matrix(first_lane, n_heads, width):
    rows = jnp.arange(SMALL_W)
    src = rows % 32
    head = src - first_lane
    valid = (rows < 96) & (head >= 0) & (head < n_heads)
    cols = jnp.arange(n_heads * width) // width
    return (valid[:, None] & (head[:, None] == cols[None, :])).astype(BF16)


def _block_tri(tb):
    i = jnp.arange(tb)
    return ((i[:, None] // CHUNK == i[None, :] // CHUNK) & (i[None, :] <= i[:, None])).astype(BF16)


def _all_layer_weights(norm_w, w_in, pool_w, pool_scale, ssd_conv_w, ssd_conv_b, ssd_dt_bias, ssd_A_log,
                       ssd_D, ssd_norm_w, gdn_conv_w, gdn_dt_bias, gdn_A_log, gdn_norm_w, w_br_pool,
                       w_br_ssd, w_br_gdn, w_out):
    n_pad = SMALL_W - SSD_H - 2 * GDN_HV
    w_a = w_in[:, :, 0:IN_DT].astype(BF16)
    w_gdn = w_in[:, :, IN_GDN:IN_BETA].astype(BF16)
    w_gate = w_in[:, :, IN_GATE:IN_END].astype(BF16)
    w_small = jnp.concatenate([w_in[:, :, IN_DT:IN_GDN], w_in[:, :, IN_BETA:IN_GATE],
                               jnp.zeros((DEPTH, D_MODEL, n_pad), F32)], axis=2).astype(BF16)
    pad8 = jnp.zeros((DEPTH, GDN_HV), F32)
    pad = jnp.zeros((DEPTH, n_pad), F32)
    sbias = jnp.concatenate([ssd_dt_bias, pad8, gdn_dt_bias, pad], axis=1)[:, None, :]
    salog = jnp.concatenate([ssd_A_log, pad8, gdn_A_log, pad], axis=1)[:, None, :]
    return [
        norm_w[:, None, :], w_a, w_small, w_gdn, w_gate, pool_w.astype(BF16), pool_scale[:, None, :],
        ssd_conv_w, ssd_conv_b[:, None, :], sbias, salog,
        jnp.repeat(ssd_D, SSD_P, axis=1)[:, None, :], ssd_norm_w[:, None, :],
        gdn_conv_w, jnp.tile(gdn_norm_w, (1, GDN_HV))[:, None, :],
        jnp.stack([w_br_pool, w_br_ssd, w_br_gdn], axis=1).astype(BF16), w_out.astype(BF16),
    ]


def kernel(x_prompt, x_sample, state_pool, state_ssd_conv, state_ssd, state_gdn_conv, state_gdn, norm_w, w_in,
           pool_w, pool_scale, ssd_conv_w, ssd_conv_b, ssd_dt_bias, ssd_A_log, ssd_D, ssd_norm_w, gdn_conv_w,
           gdn_dt_bias, gdn_A_log, gdn_norm_w, w_br_pool, w_br_ssd, w_br_gdn, w_out, final_norm_w):
    bp, seq, _ = x_prompt.shape
    bs, dseq, _ = x_sample.shape
    assert bp == 1 and dseq == CHUNK and seq % PROMPT_BLOCK == 0 and bs % SAMPLE_SEQS == 0
    e_mats = [_expand_matrix(0, SSD_H, SSD_P), _expand_matrix(SSD_H, GDN_HV, GDN_DK),
              _expand_matrix(SSD_H + GDN_HV, GDN_HV, GDN_DK)]
    consts_p = [_block_tri(PROMPT_BLOCK // PROMPT_SUB_BLOCKS)] + e_mats
    consts_s = [_block_tri(SAMPLE_SEQS * CHUNK)] + e_mats
    weights = _all_layer_weights(norm_w, w_in, pool_w, pool_scale, ssd_conv_w, ssd_conv_b, ssd_dt_bias,
                                 ssd_A_log, ssd_D, ssd_norm_w, gdn_conv_w, gdn_dt_bias, gdn_A_log, gdn_norm_w,
                                 w_br_pool, w_br_ssd, w_br_gdn, w_out)
    fnw = final_norm_w[None, :]
    states_s = (state_pool, state_ssd_conv, state_ssd.reshape(DEPTH, bs, D_MODEL, SSD_N),
                state_gdn_conv, state_gdn)

    hp = x_prompt.reshape(seq, D_MODEL)
    hs = x_sample.reshape(bs * dseq, D_MODEL)
    stk_p = [jnp.zeros((DEPTH, 1) + s, F32) for s in STATE_SHAPES]
    stk_s = [jnp.zeros((DEPTH, bs) + s, F32) for s in STATE_SHAPES]
    for l in range(DEPTH):
        last = l == DEPTH - 1
        rp = _run_layer(l, hp, None, weights, fnw, consts_p, stk_p, prompt=True, final_norm=last)
        rs = _run_layer(l, hs, states_s, weights, fnw, consts_s, stk_s, prompt=False, final_norm=last)
        hp, stk_p = rp[0], rp[1:]
        hs, stk_s = rs[0], rs[1:]

    def ssd_out(st):
        return st.reshape(DEPTH, st.shape[1], SSD_H, SSD_P, SSD_N)

    return (hp.reshape(bp, seq, D_MODEL), hs.reshape(bs, dseq, D_MODEL),
            stk_p[0], stk_s[0], stk_p[1], stk_s[1], ssd_out(stk_p[2]), ssd_out(stk_s[2]),
            stk_p[3], stk_s[3], stk_p[4], stk_s[4])
```
